```python
import jax, jax.numpy as jnp
from jax import lax
import numpy as np

D_MODEL = 1024
BATCH = 4
SEQ = 4096
DEPTH = 4

N_HEADS_MLA = 8
QK_NOPE = 64
QK_ROPE = 32
V_HEAD = 64
Q_RANK = 384
KV_RANK = 256
ROPE_THETA = 10000.0
Q_BLOCK = 128
MLA_WIDTH = N_HEADS_MLA * V_HEAD
POOL_WINDOWS = (2, 4, 8, 16)
POOL_GROUPS = 4
POOL_GROUP = 128
POOL_WIDTH = POOL_GROUPS * POOL_GROUP
POOL_OUT_GROUP = D_MODEL // POOL_GROUPS
CONV_HEADS = 8
CONV_HEAD_DIM = 64
CONV_WIDTH = CONV_HEADS * CONV_HEAD_DIM
CONV_K = 3
N_BRANCH = 3
N_EXPERTS = 16
EXPERT_FF = 1024
EC_FACTOR = 2
EPS = 1e-6

IN_SPLITS = (Q_RANK, KV_RANK, QK_ROPE, POOL_WIDTH, CONV_WIDTH, CONV_WIDTH, CONV_WIDTH, N_BRANCH * D_MODEL)
IN_COLS = sum(IN_SPLITS)
IN_OFFSETS = tuple(int(v) for v in np.cumsum(IN_SPLITS)[:-1])

kernel_name = "hybrid_mla_pool_conv_ec_moe_encoder"


def rmsnorm(x, g):
    xf = x.astype(jnp.float32)
    y = xf * lax.rsqrt(jnp.mean(xf * xf, axis=-1, keepdims=True) + EPS)
    return (y * g.astype(jnp.float32)).astype(x.dtype)


def rope_tables(positions, dim):
    freqs = ROPE_THETA ** (-jnp.arange(0, dim, 2, dtype=jnp.float32) / dim)
    ang = positions.astype(jnp.float32)[..., None] * freqs
    return jnp.cos(ang), jnp.sin(ang)


def apply_rope(x, cos, sin):
    half = x.shape[-1] // 2
    xf = x.astype(jnp.float32)
    x1, x2 = xf[..., :half], xf[..., half:]
    return jnp.concatenate([x1 * cos - x2 * sin, x1 * sin + x2 * cos], axis=-1).astype(x.dtype)


def mla_branch(cq, ckv, kr, cos, sin, q_norm_g, w_uq, kv_norm_g, w_ukv, w_oa):
    B, S, _ = cq.shape
    H, DQK = N_HEADS_MLA, QK_NOPE + QK_ROPE
    q = (rmsnorm(cq, q_norm_g) @ w_uq).reshape(B, S, H, DQK)
    q_rope = apply_rope(q[..., QK_NOPE:], cos[:, :, None], sin[:, :, None])
    kv = (rmsnorm(ckv, kv_norm_g) @ w_ukv).reshape(B, S, H, QK_NOPE + V_HEAD)
    k_nope, v = kv[..., :QK_NOPE], kv[..., QK_NOPE:]
    k_rope = apply_rope(kr, cos, sin)
    q = jnp.concatenate([q[..., :QK_NOPE], q_rope], axis=-1) * (DQK ** -0.5)
    k = jnp.concatenate([k_nope, jnp.broadcast_to(k_rope[:, :, None], (B, S, H, QK_ROPE))], axis=-1)
    nb = S // Q_BLOCK
    qb = q.reshape(B, nb, Q_BLOCK, H, DQK).transpose(1, 0, 2, 3, 4)

    def attend(q_blk):
        s = jnp.einsum('bqhd,bkhd->bhqk', q_blk, k).astype(jnp.float32)
        p = jax.nn.softmax(s, axis=-1).astype(v.dtype)
        return jnp.einsum('bhqk,bkhd->bqhd', p, v)

    o = lax.map(attend, qb)
    o = o.transpose(1, 0, 2, 3, 4).reshape(B, S, MLA_WIDTH)
    return o @ w_oa


def pool_branch(u, w_pool, pool_scale):
    B, S, _ = u.shape
    ug = u.reshape(B, S, POOL_GROUPS, POOL_GROUP).astype(jnp.float32)
    cs = jnp.concatenate([jnp.zeros((B, 1, POOL_GROUPS, POOL_GROUP), jnp.float32),
                          jnp.cumsum(ug, axis=1)], axis=1)
    t = jnp.arange(S)
    outs = []
    for gi, w in enumerate(POOL_WINDOWS):
        lo = jnp.clip(t - w // 2, 0, S - 1)
        hi = jnp.clip(t + w // 2 - 1, 0, S - 1)
        cnt = (hi - lo + 1).astype(jnp.float32)[:, None]
        csg = cs[:, :, gi]
        mean = (csg[:, hi + 1] - csg[:, lo]) / cnt
        outs.append(mean - ug[:, :, gi])
    mixed = jnp.stack(outs, axis=2).astype(u.dtype)
    y = jnp.einsum('bsgc,gcd->bsgd', mixed, w_pool).reshape(B, S, D_MODEL)
    return y * pool_scale


def conv_branch(xin, b_gate, c_gate, conv_w, w_oc):
    S = xin.shape[1]
    z = c_gate * xin
    pad = CONV_K // 2
    zp = jnp.pad(z, ((0, 0), (pad, pad), (0, 0)))
    y = sum(conv_w[k] * zp[:, k:k + S] for k in range(CONV_K))
    return (b_gate * y) @ w_oc


def ec_moe(h, w_router, w_gate, w_up, w_down):
    B, S, D = h.shape
    cap = EC_FACTOR * S // N_EXPERTS
    logits = jnp.einsum('bsd,de->bse', h, w_router).astype(jnp.float32)
    aff = jax.nn.softmax(logits, axis=-1)
    top_val, top_idx = lax.top_k(aff.transpose(0, 2, 1), cap)
    xg = jax.vmap(lambda hb, ib: hb[ib])(h, top_idx)
    g = jnp.einsum('becd,edf->becf', xg, w_gate)
    u = jnp.einsum('becd,edf->becf', xg, w_up)
    y = jnp.einsum('becf,efd->becd', jax.nn.silu(g) * u, w_down) * top_val[..., None].astype(h.dtype)
    return jax.vmap(lambda yb, ib: jnp.zeros((S, D), yb.dtype).at[ib.reshape(-1)].add(yb.reshape(-1, D)))(y, top_idx)


def setup_inputs(seed: int = 0) -> dict:
    key = jax.random.key(seed)
    ks = jax.random.split(key, 24)
    f32 = jnp.float32
    D, L = D_MODEL, DEPTH

    def nrm(k, shape, scale):
        return jax.random.normal(k, shape, f32) * scale

    steps = jax.random.randint(ks[2], (BATCH, SEQ), 1, 3)
    positions = (jnp.cumsum(steps, axis=1) - steps[:, :1]).astype(jnp.int32)
    return {
        "x": nrm(ks[0], (BATCH, SEQ, D), 1.0),
        "c": nrm(ks[1], (BATCH, D), 1.0),
        "positions": positions,
        "w_mod": nrm(ks[3], (L, D, 6 * D), 0.5 * D ** -0.5),
        "b_mod": nrm(ks[4], (L, 6 * D), 0.01),
        "norm1_g": 1.0 + nrm(ks[5], (L, D), 0.05),
        "w_in": nrm(ks[6], (L, D, IN_COLS), D ** -0.5),
        "b_gate": nrm(ks[7], (L, N_BRANCH * D), 0.01),
        "q_norm_g": 1.0 + nrm(ks[8], (L, Q_RANK), 0.05),
        "w_uq": nrm(ks[9], (L, Q_RANK, N_HEADS_MLA * (QK_NOPE + QK_ROPE)), Q_RANK ** -0.5),
        "kv_norm_g": 1.0 + nrm(ks[10], (L, KV_RANK), 0.05),
        "w_ukv": nrm(ks[11], (L, KV_RANK, N_HEADS_MLA * (QK_NOPE + V_HEAD)), KV_RANK ** -0.5),
        "w_oa": nrm(ks[12], (L, MLA_WIDTH, D), MLA_WIDTH ** -0.5),
        "w_pool": nrm(ks[13], (L, POOL_GROUPS, POOL_GROUP, POOL_OUT_GROUP), POOL_GROUP ** -0.5),
        "pool_scale": 1.0 + nrm(ks[14], (L, D), 0.05),
        "conv_w": nrm(ks[15], (L, CONV_K, CONV_WIDTH), CONV_K ** -0.5),
        "w_oc": nrm(ks[16], (L, CONV_WIDTH, D), CONV_WIDTH ** -0.5),
        "w_out": nrm(ks[17], (L, D, D), D ** -0.5),
        "norm2_g": 1.0 + nrm(ks[18], (L, D), 0.05),
        "w_router": nrm(ks[19], (L, D, N_EXPERTS), D ** -0.5),
        "w_gate": nrm(ks[20], (L, N_EXPERTS, D, EXPERT_FF), D ** -0.5),
        "w_up": nrm(ks[21], (L, N_EXPERTS, D, EXPERT_FF), D ** -0.5),
        "w_down": nrm(ks[22], (L, N_EXPERTS, EXPERT_FF, D), EXPERT_FF ** -0.5),
        "final_g": 1.0 + nrm(ks[23], (D,), 0.05),
    }


def reference(x, c, positions, w_mod, b_mod, norm1_g, w_in, b_gate, q_norm_g, w_uq, kv_norm_g, w_ukv, w_oa,
              w_pool, pool_scale, conv_w, w_oc, w_out, norm2_g, w_router, w_gate, w_up, w_down, final_g):
    B, S, D = x.shape
    cos, sin = rope_tables(positions, QK_ROPE)
    c_act = jax.nn.silu(c)
    for l in range(DEPTH):
        mod = (c_act @ w_mod[l] + b_mod[l])[:, None, :]
        sh1, sc1, g1, sh2, sc2, g2 = jnp.split(mod, 6, axis=-1)
        h = rmsnorm(x, norm1_g[l]) * (1.0 + sc1) + sh1
        proj = h @ w_in[l]
        cq, ckv, kr, pu, cx, cb, cc, gl = jnp.split(proj, IN_OFFSETS, axis=-1)
        ya = mla_branch(cq, ckv, kr, cos, sin, q_norm_g[l], w_uq[l], kv_norm_g[l], w_ukv[l], w_oa[l])
        yb = pool_branch(pu, w_pool[l], pool_scale[l])
        yc = conv_branch(cx, cb, cc, conv_w[l], w_oc[l])
        gates = jax.nn.sigmoid((gl + b_gate[l]).astype(jnp.float32)).astype(x.dtype).reshape(B, S, N_BRANCH, D)
        merged = gates[:, :, 0] * ya + gates[:, :, 1] * yb + gates[:, :, 2] * yc
        x = x + g1 * (merged @ w_out[l])
        h2 = rmsnorm(x, norm2_g[l]) * (1.0 + sc2) + sh2
        x = x + g2 * ec_moe(h2, w_router[l], w_gate[l], w_up[l], w_down[l])
    return rmsnorm(x, final_g)
```

```python
import functools

import jax
import jax.numpy as jnp
import numpy as np
from jax import lax
from jax.experimental import pallas as pl
from jax.experimental.pallas import tpu as pltpu

N_HEADS = 8
QK_NOPE = 64
QK_ROPE = 32
V_HEAD = 64
Q_RANK = 384
KV_RANK = 256
ROPE_THETA = 10000.0
POOL_WINDOWS = (2, 4, 8, 16)
POOL_GROUP = 128
CONV_K = 3
N_BRANCH = 3
N_EXPERTS = 16
EC_FACTOR = 2
EPS = 1e-6

LANES = 128
HEAD_PAD = 128
HALO = 16
SLOT_BLOCK = 128
TOKEN_BLOCK = 256
CNT_PAD = 32
VMEM_LIMIT = 56 * 1024 * 1024

BF16 = jnp.bfloat16
F32 = jnp.float32


def _cparams(*sem):
    return pltpu.CompilerParams(dimension_semantics=sem, vmem_limit_bytes=VMEM_LIMIT)


def _rms(x, g):
    return x * lax.rsqrt(jnp.mean(x * x, axis=-1, keepdims=True) + EPS) * g


def _mod_kernel(c_ref, w_ref, b_ref, o_ref):
    c = c_ref[...]
    c_act = c * jax.nn.sigmoid(c)
    o_ref[...] = jnp.dot(c_act, w_ref[...], preferred_element_type=F32,
                         precision=lax.Precision.HIGHEST) + b_ref[...]


def _modulation(c, w_mod, b_mod):
    L, D, N = w_mod.shape
    B = c.shape[0]
    tn = 1536
    return pl.pallas_call(
        _mod_kernel,
        grid=(L, N // tn),
        in_specs=[pl.BlockSpec((B, D), lambda l, j: (0, 0)),
                  pl.BlockSpec((None, D, tn), lambda l, j: (l, 0, j)),
                  pl.BlockSpec((None, 1, tn), lambda l, j: (l, 0, j))],
        out_specs=pl.BlockSpec((None, B, tn), lambda l, j: (l, 0, j)),
        out_shape=jax.ShapeDtypeStruct((L, B, N), F32),
        compiler_params=_cparams("parallel", "parallel"),
        name="modulation",
    )(c, w_mod, b_mod.reshape(L, 1, N))


def _rope_kernel(pos_ref, freq_ref, ck_ref, sk_ref):
    ang = pos_ref[...].astype(F32) * freq_ref[...]
    lane = lax.broadcasted_iota(jnp.int32, ang.shape, 1)
    cosv, sinv = jnp.cos(ang), jnp.sin(ang)
    half = QK_ROPE // 2
    in_rope = (lane >= QK_NOPE) & (lane < QK_NOPE + QK_ROPE)
    ck_ref[...] = jnp.where(in_rope, cosv, 0.0)
    sk_ref[...] = jnp.where(in_rope, jnp.where(lane < QK_NOPE + half, -sinv, sinv), 0.0)


def _rope_tables(positions):
    B, S = positions.shape
    half = QK_ROPE // 2
    freqs = ROPE_THETA ** (-jnp.arange(0, QK_ROPE, 2, dtype=F32) / QK_ROPE)
    freq_row = jnp.concatenate([jnp.zeros((QK_NOPE,), F32), freqs, freqs,
                                jnp.zeros((HEAD_PAD - QK_NOPE - 2 * half,), F32)]).reshape(1, HEAD_PAD)
    ts = min(S, 1024)
    out = jax.ShapeDtypeStruct((B, S, HEAD_PAD), F32)
    return pl.pallas_call(
        _rope_kernel,
        grid=(B, S // ts),
        in_specs=[pl.BlockSpec((None, ts, 1), lambda b, i: (b, i, 0)),
                  pl.BlockSpec((1, HEAD_PAD), lambda b, i: (0, 0))],
        out_specs=[pl.BlockSpec((None, ts, HEAD_PAD), lambda b, i: (b, i, 0))] * 2,
        out_shape=[out, out],
        compiler_params=_cparams("parallel", "parallel"),
        name="rope_tables",
    )(positions.reshape(B, S, 1), freq_row)


def _attn_in_kernel(x_ref, mod_ref, n1g_ref, w1_ref, qng_ref, wqa_ref, wqb_ref, kvng_ref, wkv_ref,
                    ck_ref, sk_ref,
                    q_ref, k_ref, v_ref, pu_ref, z_ref, cb_ref):
    x = x_ref[...]
    sh1, sc1 = mod_ref[0:1, :], mod_ref[1:2, :]
    h = _rms(x, n1g_ref[...]) * (1.0 + sc1) + sh1
    p = jnp.dot(h.astype(BF16), w1_ref[...], preferred_element_type=F32)
    o_ckv = Q_RANK
    o_kra = o_ckv + KV_RANK
    o_krb = o_kra + HEAD_PAD
    o_pu = o_krb + HEAD_PAD
    width = pu_ref.shape[-1]
    o_cx, o_cb, o_cc = o_pu + width, o_pu + 2 * width, o_pu + 3 * width
    cqn = _rms(p[:, 0:o_ckv], qng_ref[...]).astype(BF16)
    ckvn = _rms(p[:, o_ckv:o_kra], kvng_ref[...]).astype(BF16)
    qa = jnp.dot(cqn, wqa_ref[...], preferred_element_type=F32)
    qb = jnp.dot(cqn, wqb_ref[...], preferred_element_type=F32)
    kvp = jnp.dot(ckvn, wkv_ref[...], preferred_element_type=F32)
    ck, sk = ck_ref[...], sk_ref[...]
    lane = lax.broadcasted_iota(jnp.int32, ck.shape, 1)
    cq_tab = jnp.where(lane < QK_NOPE, 1.0, ck)
    scale = (QK_NOPE + QK_ROPE) ** -0.5
    krope = p[:, o_kra:o_krb] * ck + p[:, o_krb:o_pu] * sk
    for hd in range(N_HEADS):
        sl = slice(hd * HEAD_PAD, (hd + 1) * HEAD_PAD)
        q_ref[hd] = ((qa[:, sl] * cq_tab + qb[:, sl] * sk) * scale).astype(BF16)
        k_ref[hd] = (kvp[:, sl] + krope).astype(BF16)
    v_ref[...] = kvp[:, N_HEADS * HEAD_PAD:].astype(BF16)
    pu_ref[...] = p[:, o_pu:o_cx]
    z_ref[...] = p[:, o_cc:o_cc + width] * p[:, o_cx:o_cb]
    cb_ref[...] = p[:, o_cb:o_cc]


def _attn_in(x, mod_l, n1g, w1, qng, wqa, wqb, kvng, wkv, ck, sk, tm):
    B, S, D = x.shape
    width = (w1.shape[1] - Q_RANK - KV_RANK - 2 * HEAD_PAD) // 4
    tok = lambda b, i: (b, i, 0)
    full = lambda b, i: (0, 0)
    hspec = pl.BlockSpec((None, N_HEADS, tm, HEAD_PAD), lambda b, i: (b, 0, i, 0))
    wspec = lambda a: pl.BlockSpec(a.shape, full)
    f32o = jax.ShapeDtypeStruct((B, S, width), F32)
    return pl.pallas_call(
        _attn_in_kernel,
        grid=(B, S // tm),
        in_specs=[pl.BlockSpec((None, tm, D), tok),
                  pl.BlockSpec((None, 6, D), lambda b, i: (b, 0, 0)),
                  wspec(n1g), wspec(w1), wspec(qng), wspec(wqa), wspec(wqb), wspec(kvng), wspec(wkv),
                  pl.BlockSpec((None, tm, HEAD_PAD), tok),
                  pl.BlockSpec((None, tm, HEAD_PAD), tok)],
        out_specs=[hspec, hspec,
                   pl.BlockSpec((None, tm, N_HEADS * V_HEAD), tok),
                   pl.BlockSpec((None, tm, width), tok),
                   pl.BlockSpec((None, tm, width), tok),
                   pl.BlockSpec((None, tm, width), tok)],
        out_shape=[jax.ShapeDtypeStruct((B, N_HEADS, S, HEAD_PAD), BF16),
                   jax.ShapeDtypeStruct((B, N_HEADS, S, HEAD_PAD), BF16),
                   jax.ShapeDtypeStruct((B, S, N_HEADS * V_HEAD), BF16),
                   f32o, f32o, f32o],
        compiler_params=_cparams("parallel", "parallel"),
        name="attn_in",
    )(x, mod_l, n1g, w1, qng, wqa, wqb, kvng, wkv, ck, sk)


def _attention_kernel(q_ref, k_ref, v_ref, o_ref):
    v = v_ref[...]
    outs = []
    for hd in range(2):
        s = lax.dot_general(q_ref[hd], k_ref[hd], (((1,), (1,)), ((), ())),
                            preferred_element_type=F32)
        m = jnp.max(s, axis=-1, keepdims=True)
        p = jnp.exp(s - m)
        l = jnp.sum(p, axis=-1, keepdims=True)
        o = jnp.dot(p.astype(BF16), v, preferred_element_type=F32)
        outs.append(o / l)
    lane = lax.broadcasted_iota(jnp.int32, outs[0].shape, 1)
    o_ref[...] = jnp.where(lane < V_HEAD, outs[0], outs[1]).astype(BF16)


def _attention(q, k, v, tq):
    B, H, S, _ = q.shape
    return pl.pallas_call(
        _attention_kernel,
        grid=(B, H // 2, S // tq),
        in_specs=[pl.BlockSpec((None, 2, tq, HEAD_PAD), lambda b, h, i: (b, h, i, 0)),
                  pl.BlockSpec((None, 2, S, HEAD_PAD), lambda b, h, i: (b, h, 0, 0)),
                  pl.BlockSpec((None, S, 2 * V_HEAD), lambda b, h, i: (b, 0, h))],
        out_specs=pl.BlockSpec((None, tq, 2 * V_HEAD), lambda b, h, i: (b, i, h)),
        out_shape=jax.ShapeDtypeStruct((B, S, H * V_HEAD), BF16),
        compiler_params=_cparams("parallel", "parallel", "parallel"),
        name="attention",
    )(q, k, v)


def _shift_rows(ext, shift, ts):
    n = ext.shape[0]
    return pltpu.roll(ext, shift % n, axis=0)[HALO:HALO + ts]


def _mix_post_kernel(x_ref, o_ref, pu_ref, pup_ref, pun_ref, z_ref, zp_ref, zn_ref, cb_ref,
                     mod_ref, n1g_ref, wg_ref, bg_ref, woa_ref, wpool_ref, pscale_ref, convw_ref, woc_ref,
                     wout_ref, n2g_ref, wrt_ref,
                     xo_ref, h2_ref, lg_ref, *, seq_len):
    i = pl.program_id(1)
    last = pl.num_programs(1) - 1
    ts, D = x_ref.shape
    x = x_ref[...]
    sh1, sc1, g1 = mod_ref[0:1, :], mod_ref[1:2, :], mod_ref[2:3, :]
    sh2, sc2 = mod_ref[3:4, :], mod_ref[4:5, :]
    h = (_rms(x, n1g_ref[...]) * (1.0 + sc1) + sh1).astype(BF16)
    gl = jnp.dot(h, wg_ref[...], preferred_element_type=F32) + bg_ref[...]
    gates = jax.nn.sigmoid(gl)

    ya = jnp.dot(o_ref[...], woa_ref[...], preferred_element_type=F32)

    first_f = jnp.where(i == 0, 0.0, 1.0)
    last_f = jnp.where(i == last, 0.0, 1.0)
    t = i * ts + lax.broadcasted_iota(jnp.int32, (ts, 1), 0)
    pu = pu_ref[...]
    pext = jnp.concatenate([pup_ref[...] * first_f, pu, pun_ref[...] * last_f], axis=0)
    yb_parts = []
    for gi, w in enumerate(POOL_WINDOWS):
        gs = slice(gi * POOL_GROUP, (gi + 1) * POOL_GROUP)
        a = pext[:, gs]
        span = 1
        while span < w:
            a = a + pltpu.roll(a, span, axis=0)
            span *= 2
        wsum = _shift_rows(a, -(w // 2 - 1), ts)
        lo = jnp.maximum(t - w // 2, 0)
        hi = jnp.minimum(t + w // 2 - 1, seq_len - 1)
        cnt = (hi - lo + 1).astype(F32)
        mixed = wsum / cnt - pu[:, gs]
        yb_parts.append(jnp.dot(mixed.astype(BF16), wpool_ref[gi], preferred_element_type=F32))
    yb = jnp.concatenate(yb_parts, axis=-1) * pscale_ref[...]

    z = z_ref[...]
    zext = jnp.concatenate([zp_ref[...] * first_f, z, zn_ref[...] * last_f], axis=0)
    yconv = (convw_ref[0:1, :] * _shift_rows(zext, 1, ts) + convw_ref[1:2, :] * z
             + convw_ref[2:3, :] * _shift_rows(zext, -1, ts))
    yc = jnp.dot((cb_ref[...] * yconv).astype(BF16), woc_ref[...], preferred_element_type=F32)

    merged = gates[:, 0:D] * ya + gates[:, D:2 * D] * yb + gates[:, 2 * D:3 * D] * yc
    xn = x + g1 * jnp.dot(merged.astype(BF16), wout_ref[...], preferred_element_type=F32)
    xo_ref[...] = xn
    h2 = _rms(xn, n2g_ref[...]) * (1.0 + sc2) + sh2
    h2_ref[...] = h2.astype(BF16)
    lg_ref[...] = lax.dot_general(wrt_ref[...], h2, (((1,), (1,)), ((), ())),
                                  preferred_element_type=F32, precision=lax.Precision.HIGHEST)


def _mix_post(x, o, pu, z, cb, mod_l, n1g, wg, bg, woa, wpool, pscale, convw, woc, wout, n2g, wrt, ts):
    B, S, D = x.shape
    W = pu.shape[-1]
    E = wrt.shape[0]
    nh = ts // HALO
    tok = lambda b, i: (b, i, 0)
    prev = lambda b, i: (b, jnp.maximum(i * nh - 1, 0), 0)
    nxt = lambda b, i: (b, jnp.minimum((i + 1) * nh, S // HALO - 1), 0)
    wspec = lambda a: pl.BlockSpec(a.shape, lambda b, i: (0,) * a.ndim)
    return pl.pallas_call(
        functools.partial(_mix_post_kernel, seq_len=S),
        grid=(B, S // ts),
        in_specs=[pl.BlockSpec((None, ts, D), tok),
                  pl.BlockSpec((None, ts, o.shape[-1]), tok),
                  pl.BlockSpec((None, ts, W), tok),
                  pl.BlockSpec((None, HALO, W), prev),
                  pl.BlockSpec((None, HALO, W), nxt),
                  pl.BlockSpec((None, ts, W), tok),
                  pl.BlockSpec((None, HALO, W), prev),
                  pl.BlockSpec((None, HALO, W), nxt),
                  pl.BlockSpec((None, ts, W), tok),
                  pl.BlockSpec((None, 6, D), lambda b, i: (b, 0, 0)),
                  wspec(n1g), wspec(wg), wspec(bg), wspec(woa), wspec(wpool), wspec(pscale), wspec(convw),
                  wspec(woc), wspec(wout), wspec(n2g), wspec(wrt)],
        out_specs=[pl.BlockSpec((None, ts, D), tok),
                   pl.BlockSpec((None, ts, D), tok),
                   pl.BlockSpec((None, E, ts), lambda b, i: (b, 0, i))],
        out_shape=[jax.ShapeDtypeStruct((B, S, D), F32),
                   jax.ShapeDtypeStruct((B, S, D), BF16),
                   jax.ShapeDtypeStruct((B, E, S), F32)],
        compiler_params=_cparams("parallel", "parallel"),
        name="mix_post",
    )(x, o, pu, pu, pu, z, z, z, cb, mod_l, n1g, wg, bg, woa, wpool, pscale, convw, woc, wout, n2g, wrt)


def _prefix_chunks(flags, tri):
    E, S = flags.shape
    running = jnp.zeros((E, 1), F32)
    pieces, starts = [], []
    for c in range(S // TOKEN_BLOCK):
        blk = flags[:, c * TOKEN_BLOCK:(c + 1) * TOKEN_BLOCK]
        incl = jnp.dot(blk.astype(BF16), tri, preferred_element_type=F32)
        pieces.append(incl - blk + running)
        starts.append(running)
        running = running + jnp.sum(blk, axis=-1, keepdims=True)
    starts.append(running)
    return jnp.concatenate(pieces, axis=-1), starts


def _route_kernel(lg_ref, slot_ref, wts_ref, cnt_ref, *, cap):
    lg = lg_ref[...]
    E, S = lg.shape
    m = jnp.max(lg, axis=0, keepdims=True)
    ex = jnp.exp(lg - m)
    aff = ex / jnp.sum(ex, axis=0, keepdims=True)

    def step(it, tau_bits):
        cand = tau_bits | jnp.left_shift(jnp.int32(1), 30 - it)
        n = jnp.sum((aff >= lax.bitcast_convert_type(cand, F32)).astype(jnp.int32), axis=-1, keepdims=True)
        return jnp.where(n >= cap, cand, tau_bits)

    tau = lax.bitcast_convert_type(lax.fori_loop(0, 31, step, jnp.zeros((E, 1), jnp.int32)), F32)
    gt = aff > tau
    eq = aff == tau
    need = cap - jnp.sum(gt.astype(jnp.int32), axis=-1, keepdims=True)
    row = lax.broadcasted_iota(jnp.int32, (TOKEN_BLOCK, TOKEN_BLOCK), 0)
    col = lax.broadcasted_iota(jnp.int32, (TOKEN_BLOCK, TOKEN_BLOCK), 1)
    tri = jnp.where(row <= col, 1.0, 0.0).astype(BF16)
    eq_rank, _ = _prefix_chunks(jnp.where(eq, 1.0, 0.0), tri)
    sel = gt | (eq & (eq_rank.astype(jnp.int32) < need))
    pos, starts = _prefix_chunks(jnp.where(sel, 1.0, 0.0), tri)
    slot_ref[...] = jnp.where(sel, pos.astype(jnp.int32), -1)
    wts_ref[...] = jnp.where(sel, aff, 0.0)
    lane = lax.broadcasted_iota(jnp.int32, (E, LANES), 1)
    cnt = jnp.full((E, LANES), cap, jnp.int32)
    for c, st in enumerate(starts):
        cnt = jnp.where(lane == c, st.astype(jnp.int32), cnt)
    cnt_ref[...] = cnt


def _route(logits_t, cap):
    B, E, S = logits_t.shape
    spec = pl.BlockSpec((None, E, S), lambda b: (b, 0, 0))
    return pl.pallas_call(
        functools.partial(_route_kernel, cap=cap),
        grid=(B,),
        in_specs=[spec],
        out_specs=[spec, spec, pl.BlockSpec((None, E, LANES), lambda b: (b, 0, 0))],
        out_shape=[jax.ShapeDtypeStruct((B, E, S), jnp.int32),
                   jax.ShapeDtypeStruct((B, E, S), F32),
                   jax.ShapeDtypeStruct((B, E, LANES), jnp.int32)],
        compiler_params=_cparams("parallel"),
        name="route",
    )(logits_t)


def _gather_kernel(cnt_ref, slot_ref, h2_ref, xg_ref, acc_ref):
    b, e = pl.program_id(0), pl.program_id(1)
    base = (b * pl.num_programs(1) + e) * CNT_PAD
    cap = xg_ref.shape[0]
    S = h2_ref.shape[0]
    acc_ref[...] = jnp.zeros_like(acc_ref)
    for c in range(S // TOKEN_BLOCK):
        c0, c1 = cnt_ref[base + c], cnt_ref[base + c + 1]

        @pl.when(c1 > c0)
        def _():
            srow = slot_ref[:, c * TOKEN_BLOCK:(c + 1) * TOKEN_BLOCK]
            htile = h2_ref[c * TOKEN_BLOCK:(c + 1) * TOKEN_BLOCK, :]
            for j in range(cap // SLOT_BLOCK):
                @pl.when((c0 < (j + 1) * SLOT_BLOCK) & (c1 > j * SLOT_BLOCK))
                def _():
                    p = lax.broadcasted_iota(jnp.int32, (SLOT_BLOCK, TOKEN_BLOCK), 0) + j * SLOT_BLOCK
                    onehot = jnp.where(p == srow, 1.0, 0.0).astype(BF16)
                    acc_ref[j * SLOT_BLOCK:(j + 1) * SLOT_BLOCK, :] += jnp.dot(
                        onehot, htile, preferred_element_type=F32)
    xg_ref[...] = acc_ref[...].astype(BF16)


def _gather(cnt_flat, slot, h2, cap):
    B, E, S = slot.shape
    D = h2.shape[-1]
    return pl.pallas_call(
        _gather_kernel,
        grid_spec=pltpu.PrefetchScalarGridSpec(
            num_scalar_prefetch=1,
            grid=(B, E),
            in_specs=[pl.BlockSpec((None, None, 1, S), lambda b, e, cnt: (b, e, 0, 0)),
                      pl.BlockSpec((None, S, D), lambda b, e, cnt: (b, 0, 0))],
            out_specs=pl.BlockSpec((None, None, cap, D), lambda b, e, cnt: (b, e, 0, 0)),
            scratch_shapes=[pltpu.VMEM((cap, D), F32)]),
        out_shape=jax.ShapeDtypeStruct((B, E, cap, D), BF16),
        compiler_params=_cparams("parallel", "arbitrary"),
        name="gather",
    )(cnt_flat, slot.reshape(B, E, 1, S), h2)


def _experts_kernel(xg_ref, wg_ref, wu_ref, wd_ref, y_ref, wgb, wub, wdb):
    @pl.when(pl.program_id(1) == 0)
    def _():
        wgb[...] = wg_ref[...].astype(BF16)
        wub[...] = wu_ref[...].astype(BF16)
        wdb[...] = wd_ref[...].astype(BF16)

    xg = xg_ref[...]
    g = jnp.dot(xg, wgb[...], preferred_element_type=F32)
    u = jnp.dot(xg, wub[...], preferred_element_type=F32)
    a = (g * jax.nn.sigmoid(g) * u).astype(BF16)
    y_ref[...] = jnp.dot(a, wdb[...], preferred_element_type=F32).astype(BF16)


def _experts(xg, w_gate, w_up, w_down, layer):
    B, E, cap, D = xg.shape
    F = w_gate.shape[-1]
    tok = pl.BlockSpec((None, None, cap, D), lambda e, b: (b, e, 0, 0))
    return pl.pallas_call(
        _experts_kernel,
        grid=(E, B),
        in_specs=[tok,
                  pl.BlockSpec((None, None, D, F), lambda e, b: (layer, e, 0, 0)),
                  pl.BlockSpec((None, None, D, F), lambda e, b: (layer, e, 0, 0)),
                  pl.BlockSpec((None, None, F, D), lambda e, b: (layer, e, 0, 0))],
        out_specs=tok,
        out_shape=jax.ShapeDtypeStruct((B, E, cap, D), BF16),
        scratch_shapes=[pltpu.VMEM((D, F), BF16), pltpu.VMEM((D, F), BF16), pltpu.VMEM((F, D), BF16)],
        compiler_params=_cparams("arbitrary", "arbitrary"),
        name="experts",
    )(xg, w_gate, w_up, w_down)


def _combine_kernel(cnt_ref, slot_ref, wts_ref, y_ref, x_ref, mod_ref, fg_ref, o_ref, acc_ref, *, final_norm):
    b, c = pl.program_id(0), pl.program_id(1)
    E, cap, D = y_ref.shape
    acc_ref[...] = jnp.zeros_like(acc_ref)
    for e in range(E):
        base = (b * E + e) * CNT_PAD + c
        c0, c1 = cnt_ref[base], cnt_ref[base + 1]

        @pl.when(c1 > c0)
        def _():
            scol = slot_ref[:, e:e + 1]
            wcol = wts_ref[:, e:e + 1]
            for j in range(cap // SLOT_BLOCK):
                @pl.when((c0 < (j + 1) * SLOT_BLOCK) & (c1 > j * SLOT_BLOCK))
                def _():
                    lane = lax.broadcasted_iota(jnp.int32, (TOKEN_BLOCK, SLOT_BLOCK), 1) + j * SLOT_BLOCK
                    onehot_t = jnp.where(lane == scol, 1.0, 0.0).astype(BF16)
                    acc_ref[...] += wcol * jnp.dot(onehot_t, y_ref[e, j * SLOT_BLOCK:(j + 1) * SLOT_BLOCK, :],
                                                   preferred_element_type=F32)
    out = x_ref[...] + mod_ref[5:6, :] * acc_ref[...]
    if final_norm:
        out = _rms(out, fg_ref[...])
    o_ref[...] = out


def _combine(cnt_flat, slot_t, wts_t, y, x, mod_l, final_g, final_norm):
    B, S, D = x.shape
    E, cap = y.shape[1], y.shape[2]
    tok = lambda b, c, cnt: (b, c, 0)
    return pl.pallas_call(
        functools.partial(_combine_kernel, final_norm=final_norm),
        grid_spec=pltpu.PrefetchScalarGridSpec(
            num_scalar_prefetch=1,
            grid=(B, S // TOKEN_BLOCK),
            in_specs=[pl.BlockSpec((None, TOKEN_BLOCK, E), tok),
                      pl.BlockSpec((None, TOKEN_BLOCK, E), tok),
                      pl.BlockSpec((None, E, cap, D), lambda b, c, cnt: (b, 0, 0, 0)),
                      pl.BlockSpec((None, TOKEN_BLOCK, D), tok),
                      pl.BlockSpec((None, 6, D), lambda b, c, cnt: (b, 0, 0)),
                      pl.BlockSpec((1, D), lambda b, c, cnt: (0, 0))],
            out_specs=pl.BlockSpec((None, TOKEN_BLOCK, D), tok),
            scratch_shapes=[pltpu.VMEM((TOKEN_BLOCK, D), F32)]),
        out_shape=jax.ShapeDtypeStruct((B, S, D), F32),
        compiler_params=_cparams("parallel", "arbitrary"),
        name="combine",
    )(cnt_flat, slot_t, wts_t, y, x, mod_l, final_g)


def _prep_layer(l, w_in, w_uq, w_ukv, w_oa, w_pool, w_oc, w_out, w_router, width):
    D = w_in.shape[1]
    half = QK_ROPE // 2
    o_ckv = Q_RANK
    o_kr = o_ckv + KV_RANK
    o_pu = o_kr + QK_ROPE
    o_gl = o_pu + 4 * width
    wi = w_in[l]
    kr = wi[:, o_kr:o_pu]
    z64 = jnp.zeros((D, QK_NOPE), F32)
    z32 = jnp.zeros((D, HEAD_PAD - QK_NOPE - QK_ROPE), F32)
    kr_a = jnp.concatenate([z64, kr, z32], axis=1)
    kr_b = jnp.concatenate([z64, kr[:, half:], kr[:, :half], z32], axis=1)
    w1 = jnp.concatenate([wi[:, :o_kr], kr_a, kr_b, wi[:, o_pu:o_gl]], axis=1).astype(BF16)
    wg = wi[:, o_gl:].astype(BF16)

    dqk = QK_NOPE + QK_ROPE
    q3 = w_uq[l].reshape(Q_RANK, N_HEADS, dqk)
    zq = jnp.zeros((Q_RANK, N_HEADS, HEAD_PAD - dqk), F32)
    wqa = jnp.concatenate([q3, zq], axis=-1).reshape(Q_RANK, N_HEADS * HEAD_PAD).astype(BF16)
    wqb = jnp.concatenate([jnp.zeros((Q_RANK, N_HEADS, QK_NOPE), F32), q3[..., QK_NOPE + half:],
                           q3[..., QK_NOPE:QK_NOPE + half], zq], axis=-1)
    wqb = wqb.reshape(Q_RANK, N_HEADS * HEAD_PAD).astype(BF16)

    kv3 = w_ukv[l].reshape(KV_RANK, N_HEADS, QK_NOPE + V_HEAD)
    wk = jnp.concatenate([kv3[..., :QK_NOPE], jnp.zeros((KV_RANK, N_HEADS, HEAD_PAD - QK_NOPE), F32)], axis=-1)
    wkv = jnp.concatenate([wk.reshape(KV_RANK, N_HEADS * HEAD_PAD),
                           kv3[..., QK_NOPE:].reshape(KV_RANK, N_HEADS * V_HEAD)], axis=1).astype(BF16)
    return dict(w1=w1, wg=wg, wqa=wqa, wqb=wqb, wkv=wkv,
                woa=w_oa[l].astype(BF16), wpool=w_pool[l].astype(BF16), woc=w_oc[l].astype(BF16),
                wout=w_out[l].astype(BF16), wrt=w_router[l].T)


def kernel(x, c, positions, w_mod, b_mod, norm1_g, w_in, b_gate, q_norm_g, w_uq, kv_norm_g, w_ukv, w_oa, w_pool,
           pool_scale, conv_w, w_oc, w_out, norm2_g, w_router, w_gate, w_up, w_down, final_g):
    B, S, D = x.shape
    L = w_mod.shape[0]
    E = w_router.shape[-1]
    width = w_oc.shape[1]
    cap = EC_FACTOR * S // E
    assert S % TOKEN_BLOCK == 0 and cap % SLOT_BLOCK == 0 and S // TOKEN_BLOCK + 1 <= CNT_PAD
    tm = min(S, 512)
    tq = min(S, 256)

    mod = _modulation(c, w_mod, b_mod).reshape(L, B, 6, D)
    ck, sk = _rope_tables(positions)
    row = lambda a: a.reshape(1, -1)
    for l in range(L):
        w = _prep_layer(l, w_in, w_uq, w_ukv, w_oa, w_pool, w_oc, w_out, w_router, width)
        q, k, v, pu, z, cb = _attn_in(x, mod[l], row(norm1_g[l]), w["w1"], row(q_norm_g[l]), w["wqa"], w["wqb"],
                                      row(kv_norm_g[l]), w["wkv"], ck, sk, tm)
        o = _attention(q, k, v, tq)
        x, h2, logits_t = _mix_post(x, o, pu, z, cb, mod[l], row(norm1_g[l]), w["wg"], row(b_gate[l]), w["woa"],
                                    w["wpool"], row(pool_scale[l]), conv_w[l], w["woc"], w["wout"],
                                    row(norm2_g[l]), w["wrt"], tm)
        slot, wts, cnt = _route(logits_t, cap)
        cnt_flat = cnt[:, :, :CNT_PAD].reshape(-1)
        xg = _gather(cnt_flat, slot, h2, cap)
        y = _experts(xg, w_gate, w_up, w_down, l)
        x = _combine(cnt_flat, slot.transpose(0, 2, 1), wts.transpose(0, 2, 1), y, x, mod[l], row(final_g),
                     final_norm=(l == L - 1))
    return x
```

```python
import functools

import jax
import jax.numpy as jnp
import numpy as np
from jax import lax
from jax.experimental import pallas as pl
from jax.experimental.pallas import tpu as pltpu

N_HEADS = 8
QK_NOPE = 64
QK_ROPE = 32
V_HEAD = 64
Q_RANK = 384
KV_RANK = 256
ROPE_THETA = 10000.0
POOL_WINDOWS = (2, 4, 8, 16)
POOL_GROUP = 128
CONV_K = 3
N_BRANCH = 3
N_EXPERTS = 16
EC_FACTOR = 2
EPS = 1e-6

LANES = 128
HEAD_PAD = 128
HALO = 16
SLOT_WINDOW = 256
SLOT_ALIGN_LOG2 = 4
TOKEN_BLOCK = 256
CNT_PAD = 32
VMEM_LIMIT = 56 * 1024 * 1024

BF16 = jnp.bfloat16
F32 = jnp.float32


def _cparams(*sem):
    return pltpu.CompilerParams(dimension_semantics=sem, vmem_limit_bytes=VMEM_LIMIT)


def _rms(x, g):
    return x * lax.rsqrt(jnp.mean(x * x, axis=-1, keepdims=True) + EPS) * g


def _mod_kernel(c_ref, w_ref, b_ref, o_ref):
    c = c_ref[...]
    c_act = c * jax.nn.sigmoid(c)
    o_ref[...] = jnp.dot(c_act, w_ref[...], preferred_element_type=F32,
                         precision=lax.Precision.HIGHEST) + b_ref[...]


def _modulation(c, w_mod, b_mod):
    L, D, N = w_mod.shape
    B = c.shape[0]
    tn = 1536
    return pl.pallas_call(
        _mod_kernel,
        grid=(L, N // tn),
        in_specs=[pl.BlockSpec((B, D), lambda l, j: (0, 0)),
                  pl.BlockSpec((None, D, tn), lambda l, j: (l, 0, j)),
                  pl.BlockSpec((None, 1, tn), lambda l, j: (l, 0, j))],
        out_specs=pl.BlockSpec((None, B, tn), lambda l, j: (l, 0, j)),
        out_shape=jax.ShapeDtypeStruct((L, B, N), F32),
        compiler_params=_cparams("parallel", "parallel"),
        name="modulation",
    )(c, w_mod, b_mod.reshape(L, 1, N))


def _rope_kernel(pos_ref, freq_ref, ck_ref, sk_ref):
    ang = pos_ref[...].astype(F32) * freq_ref[...]
    lane = lax.broadcasted_iota(jnp.int32, ang.shape, 1)
    cosv, sinv = jnp.cos(ang), jnp.sin(ang)
    half = QK_ROPE // 2
    in_rope = (lane >= QK_NOPE) & (lane < QK_NOPE + QK_ROPE)
    ck_ref[...] = jnp.where(in_rope, cosv, 0.0)
    sk_ref[...] = jnp.where(in_rope, jnp.where(lane < QK_NOPE + half, -sinv, sinv), 0.0)


def _rope_tables(positions):
    B, S = positions.shape
    half = QK_ROPE // 2
    freqs = ROPE_THETA ** (-jnp.arange(0, QK_ROPE, 2, dtype=F32) / QK_ROPE)
    freq_row = jnp.concatenate([jnp.zeros((QK_NOPE,), F32), freqs, freqs,
                                jnp.zeros((HEAD_PAD - QK_NOPE - 2 * half,), F32)]).reshape(1, HEAD_PAD)
    ts = min(S, 1024)
    out = jax.ShapeDtypeStruct((B, S, HEAD_PAD), F32)
    return pl.pallas_call(
        _rope_kernel,
        grid=(B, S // ts),
        in_specs=[pl.BlockSpec((None, ts, 1), lambda b, i: (b, i, 0)),
                  pl.BlockSpec((1, HEAD_PAD), lambda b, i: (0, 0))],
        out_specs=[pl.BlockSpec((None, ts, HEAD_PAD), lambda b, i: (b, i, 0))] * 2,
        out_shape=[out, out],
        compiler_params=_cparams("parallel", "parallel"),
        name="rope_tables",
    )(positions.reshape(B, S, 1), freq_row)


def _attn_in_kernel(x_ref, mod_ref, n1g_ref, w1_ref, qng_ref, wqa_ref, wqb_ref, kvng_ref, wkv_ref,
                    ck_ref, sk_ref,
                    q_ref, k_ref, v_ref, pu_ref, z_ref, cb_ref):
    x = x_ref[...]
    sh1, sc1 = mod_ref[0:1, :], mod_ref[1:2, :]
    h = _rms(x, n1g_ref[...]) * (1.0 + sc1) + sh1
    p = jnp.dot(h.astype(BF16), w1_ref[...], preferred_element_type=F32)
    o_ckv = Q_RANK
    o_kra = o_ckv + KV_RANK
    o_krb = o_kra + HEAD_PAD
    o_pu = o_krb + HEAD_PAD
    width = pu_ref.shape[-1]
    o_cx, o_cb, o_cc = o_pu + width, o_pu + 2 * width, o_pu + 3 * width
    cqn = _rms(p[:, 0:o_ckv], qng_ref[...]).astype(BF16)
    ckvn = _rms(p[:, o_ckv:o_kra], kvng_ref[...]).astype(BF16)
    qa = jnp.dot(cqn, wqa_ref[...], preferred_element_type=F32)
    qb = jnp.dot(cqn, wqb_ref[...], preferred_element_type=F32)
    kvp = jnp.dot(ckvn, wkv_ref[...], preferred_element_type=F32)
    ck, sk = ck_ref[...], sk_ref[...]
    lane = lax.broadcasted_iota(jnp.int32, ck.shape, 1)
    cq_tab = jnp.where(lane < QK_NOPE, 1.0, ck)
    scale = (QK_NOPE + QK_ROPE) ** -0.5 * float(np.log2(np.e))
    krope = p[:, o_kra:o_krb] * ck + p[:, o_krb:o_pu] * sk
    for hd in range(N_HEADS):
        sl = slice(hd * HEAD_PAD, (hd + 1) * HEAD_PAD)
        q_ref[hd] = ((qa[:, sl] * cq_tab + qb[:, sl] * sk) * scale).astype(BF16)
        k_ref[hd] = (kvp[:, sl] + krope).astype(BF16)
    v_ref[...] = kvp[:, N_HEADS * HEAD_PAD:].astype(BF16)
    pu_ref[...] = p[:, o_pu:o_cx]
    z_ref[...] = p[:, o_cc:o_cc + width] * p[:, o_cx:o_cb]
    cb_ref[...] = p[:, o_cb:o_cc]


def _attn_in(x, mod_l, n1g, w1, qng, wqa, wqb, kvng, wkv, ck, sk, tm):
    B, S, D = x.shape
    width = (w1.shape[1] - Q_RANK - KV_RANK - 2 * HEAD_PAD) // 4
    tok = lambda b, i: (b, i, 0)
    full = lambda b, i: (0, 0)
    hspec = pl.BlockSpec((None, N_HEADS, tm, HEAD_PAD), lambda b, i: (b, 0, i, 0))
    wspec = lambda a: pl.BlockSpec(a.shape, full)
    f32o = jax.ShapeDtypeStruct((B, S, width), F32)
    return pl.pallas_call(
        _attn_in_kernel,
        grid=(B, S // tm),
        in_specs=[pl.BlockSpec((None, tm, D), tok),
                  pl.BlockSpec((None, 6, D), lambda b, i: (b, 0, 0)),
                  wspec(n1g), wspec(w1), wspec(qng), wspec(wqa), wspec(wqb), wspec(kvng), wspec(wkv),
                  pl.BlockSpec((None, tm, HEAD_PAD), tok),
                  pl.BlockSpec((None, tm, HEAD_PAD), tok)],
        out_specs=[hspec, hspec,
                   pl.BlockSpec((None, tm, N_HEADS * V_HEAD), tok),
                   pl.BlockSpec((None, tm, width), tok),
                   pl.BlockSpec((None, tm, width), tok),
                   pl.BlockSpec((None, tm, width), tok)],
        out_shape=[jax.ShapeDtypeStruct((B, N_HEADS, S, HEAD_PAD), BF16),
                   jax.ShapeDtypeStruct((B, N_HEADS, S, HEAD_PAD), BF16),
                   jax.ShapeDtypeStruct((B, S, N_HEADS * V_HEAD), BF16),
                   f32o, f32o, f32o],
        compiler_params=_cparams("parallel", "parallel"),
        name="attn_in",
    )(x, mod_l, n1g, w1, qng, wqa, wqb, kvng, wkv, ck, sk)


def _attention_kernel(q_ref, k_ref, v_ref, o_ref):
    nh = q_ref.shape[0]
    pair = 2 * V_HEAD
    scores = [lax.dot_general(q_ref[hd], k_ref[hd], (((1,), (1,)), ((), ())),
                              preferred_element_type=F32) for hd in range(nh)]
    outs = []
    for hd in range(nh):
        s = scores[hd]
        m = jnp.max(s, axis=-1, keepdims=True)
        p = jnp.exp2(s - m)
        l = jnp.sum(p, axis=-1, keepdims=True)
        v = v_ref[:, (hd // 2) * pair:(hd // 2 + 1) * pair]
        o = jnp.dot(p.astype(BF16), v, preferred_element_type=F32)
        outs.append(o / l)
    lane = lax.broadcasted_iota(jnp.int32, outs[0].shape, 1)
    for pr in range(nh // 2):
        o_ref[:, pr * pair:(pr + 1) * pair] = jnp.where(lane < V_HEAD, outs[2 * pr], outs[2 * pr + 1]).astype(BF16)


def _attention(q, k, v, tq, nh=4):
    B, H, S, _ = q.shape
    return pl.pallas_call(
        _attention_kernel,
        grid=(B, H // nh, S // tq),
        in_specs=[pl.BlockSpec((None, nh, tq, HEAD_PAD), lambda b, h, i: (b, h, i, 0)),
                  pl.BlockSpec((None, nh, S, HEAD_PAD), lambda b, h, i: (b, h, 0, 0)),
                  pl.BlockSpec((None, S, nh * V_HEAD), lambda b, h, i: (b, 0, h))],
        out_specs=pl.BlockSpec((None, tq, nh * V_HEAD), lambda b, h, i: (b, i, h)),
        out_shape=jax.ShapeDtypeStruct((B, S, H * V_HEAD), BF16),
        compiler_params=_cparams("parallel", "parallel", "parallel"),
        name="attention",
    )(q, k, v)


def _shift_rows(ext, shift, ts):
    n = ext.shape[0]
    return pltpu.roll(ext, shift % n, axis=0)[HALO:HALO + ts]


def _mix_post_kernel(x_ref, o_ref, pu_ref, pup_ref, pun_ref, z_ref, zp_ref, zn_ref, cb_ref,
                     mod_ref, n1g_ref, wg_ref, bg_ref, woa_ref, wpool_ref, pscale_ref, convw_ref, woc_ref,
                     wout_ref, n2g_ref, wrt_ref,
                     xo_ref, h2_ref, lg_ref, *, seq_len):
    i = pl.program_id(1)
    last = pl.num_programs(1) - 1
    ts, D = x_ref.shape
    x = x_ref[...]
    sh1, sc1, g1 = mod_ref[0:1, :], mod_ref[1:2, :], mod_ref[2:3, :]
    sh2, sc2 = mod_ref[3:4, :], mod_ref[4:5, :]
    h = (_rms(x, n1g_ref[...]) * (1.0 + sc1) + sh1).astype(BF16)
    gl = jnp.dot(h, wg_ref[...], preferred_element_type=F32) + bg_ref[...]
    gates = jax.nn.sigmoid(gl)

    ya = jnp.dot(o_ref[...], woa_ref[...], preferred_element_type=F32)

    first_f = jnp.where(i == 0, 0.0, 1.0)
    last_f = jnp.where(i == last, 0.0, 1.0)
    t = i * ts + lax.broadcasted_iota(jnp.int32, (ts, 1), 0)
    pu = pu_ref[...]
    pext = jnp.concatenate([pup_ref[...] * first_f, pu, pun_ref[...] * last_f], axis=0)
    yb_parts = []
    for gi, w in enumerate(POOL_WINDOWS):
        gs = slice(gi * POOL_GROUP, (gi + 1) * POOL_GROUP)
        a = pext[:, gs]
        span = 1
        while span < w:
            a = a + pltpu.roll(a, span, axis=0)
            span *= 2
        wsum = _shift_rows(a, -(w // 2 - 1), ts)
        lo = jnp.maximum(t - w // 2, 0)
        hi = jnp.minimum(t + w // 2 - 1, seq_len - 1)
        cnt = (hi - lo + 1).astype(F32)
        mixed = wsum / cnt - pu[:, gs]
        yb_parts.append(jnp.dot(mixed.astype(BF16), wpool_ref[gi], preferred_element_type=F32))
    yb = jnp.concatenate(yb_parts, axis=-1) * pscale_ref[...]

    z = z_ref[...]
    zext = jnp.concatenate([zp_ref[...] * first_f, z, zn_ref[...] * last_f], axis=0)
    yconv = (convw_ref[0:1, :] * _shift_rows(zext, 1, ts) + convw_ref[1:2, :] * z
             + convw_ref[2:3, :] * _shift_rows(zext, -1, ts))
    yc = jnp.dot((cb_ref[...] * yconv).astype(BF16), woc_ref[...], preferred_element_type=F32)

    merged = gates[:, 0:D] * ya + gates[:, D:2 * D] * yb + gates[:, 2 * D:3 * D] * yc
    xn = x + g1 * jnp.dot(merged.astype(BF16), wout_ref[...], preferred_element_type=F32)
    xo_ref[...] = xn
    h2 = _rms(xn, n2g_ref[...]) * (1.0 + sc2) + sh2
    h2_ref[...] = h2.astype(BF16)
    lg_ref[...] = lax.dot_general(wrt_ref[...], h2, (((1,), (1,)), ((), ())),
                                  preferred_element_type=F32, precision=lax.Precision.HIGHEST)


def _mix_post(x, o, pu, z, cb, mod_l, n1g, wg, bg, woa, wpool, pscale, convw, woc, wout, n2g, wrt, ts):
    B, S, D = x.shape
    W = pu.shape[-1]
    E = wrt.shape[0]
    nh = ts // HALO
    tok = lambda b, i: (b, i, 0)
    prev = lambda b, i: (b, jnp.maximum(i * nh - 1, 0), 0)
    nxt = lambda b, i: (b, jnp.minimum((i + 1) * nh, S // HALO - 1), 0)
    wspec = lambda a: pl.BlockSpec(a.shape, lambda b, i: (0,) * a.ndim)
    return pl.pallas_call(
        functools.partial(_mix_post_kernel, seq_len=S),
        grid=(B, S // ts),
        in_specs=[pl.BlockSpec((None, ts, D), tok),
                  pl.BlockSpec((None, ts, o.shape[-1]), tok),
                  pl.BlockSpec((None, ts, W), tok),
                  pl.BlockSpec((None, HALO, W), prev),
                  pl.BlockSpec((None, HALO, W), nxt),
                  pl.BlockSpec((None, ts, W), tok),
                  pl.BlockSpec((None, HALO, W), prev),
                  pl.BlockSpec((None, HALO, W), nxt),
                  pl.BlockSpec((None, ts, W), tok),
                  pl.BlockSpec((None, 6, D), lambda b, i: (b, 0, 0)),
                  wspec(n1g), wspec(wg), wspec(bg), wspec(woa), wspec(wpool), wspec(pscale), wspec(convw),
                  wspec(woc), wspec(wout), wspec(n2g), wspec(wrt)],
        out_specs=[pl.BlockSpec((None, ts, D), tok),
                   pl.BlockSpec((None, ts, D), tok),
                   pl.BlockSpec((None, E, ts), lambda b, i: (b, 0, i))],
        out_shape=[jax.ShapeDtypeStruct((B, S, D), F32),
                   jax.ShapeDtypeStruct((B, S, D), BF16),
                   jax.ShapeDtypeStruct((B, E, S), F32)],
        compiler_params=_cparams("parallel", "parallel"),
        name="mix_post",
    )(x, o, pu, pu, pu, z, z, z, cb, mod_l, n1g, wg, bg, woa, wpool, pscale, convw, woc, wout, n2g, wrt)


def _prefix_chunks(flags, tri):
    E, S = flags.shape
    running = jnp.zeros((E, 1), F32)
    pieces, starts = [], []
    for c in range(S // TOKEN_BLOCK):
        blk = flags[:, c * TOKEN_BLOCK:(c + 1) * TOKEN_BLOCK]
        incl = jnp.dot(blk.astype(BF16), tri, preferred_element_type=F32)
        pieces.append(incl - blk + running)
        starts.append(running)
        running = running + jnp.sum(blk, axis=-1, keepdims=True)
    starts.append(running)
    return jnp.concatenate(pieces, axis=-1), starts


def _route_kernel(lg_ref, slot_ref, wts_ref, cnt_ref, *, cap):
    lg = lg_ref[...]
    E, S = lg.shape
    m = jnp.max(lg, axis=0, keepdims=True)
    ex = jnp.exp(lg - m)
    aff = ex / jnp.sum(ex, axis=0, keepdims=True)

    def step(it, tau_bits):
        cand = tau_bits | jnp.left_shift(jnp.int32(1), 30 - it)
        n = jnp.sum((aff >= lax.bitcast_convert_type(cand, F32)).astype(jnp.int32), axis=-1, keepdims=True)
        return jnp.where(n >= cap, cand, tau_bits)

    tau = lax.bitcast_convert_type(lax.fori_loop(0, 31, step, jnp.zeros((E, 1), jnp.int32)), F32)
    gt = aff > tau
    eq = aff == tau
    need = cap - jnp.sum(gt.astype(jnp.int32), axis=-1, keepdims=True)
    row = lax.broadcasted_iota(jnp.int32, (TOKEN_BLOCK, TOKEN_BLOCK), 0)
    col = lax.broadcasted_iota(jnp.int32, (TOKEN_BLOCK, TOKEN_BLOCK), 1)
    tri = jnp.where(row <= col, 1.0, 0.0).astype(BF16)
    eq_rank, _ = _prefix_chunks(jnp.where(eq, 1.0, 0.0), tri)
    sel = gt | (eq & (eq_rank.astype(jnp.int32) < need))
    pos, starts = _prefix_chunks(jnp.where(sel, 1.0, 0.0), tri)
    slot_ref[...] = jnp.where(sel, pos.astype(jnp.int32), -1)
    wts_ref[...] = jnp.where(sel, aff, 0.0)
    lane = lax.broadcasted_iota(jnp.int32, (E, LANES), 1)
    cnt = jnp.full((E, LANES), cap, jnp.int32)
    for c, st in enumerate(starts):
        cnt = jnp.where(lane == c, st.astype(jnp.int32), cnt)
    cnt_ref[...] = cnt


def _route(logits_t, cap):
    B, E, S = logits_t.shape
    spec = pl.BlockSpec((None, E, S), lambda b: (b, 0, 0))
    return pl.pallas_call(
        functools.partial(_route_kernel, cap=cap),
        grid=(B,),
        in_specs=[spec],
        out_specs=[spec, spec, pl.BlockSpec((None, E, LANES), lambda b: (b, 0, 0))],
        out_shape=[jax.ShapeDtypeStruct((B, E, S), jnp.int32),
                   jax.ShapeDtypeStruct((B, E, S), F32),
                   jax.ShapeDtypeStruct((B, E, LANES), jnp.int32)],
        compiler_params=_cparams("parallel"),
        name="route",
    )(logits_t)


def _slot_windows(c0, cap, win):
    a0 = jnp.minimum(jnp.left_shift(jnp.right_shift(c0, SLOT_ALIGN_LOG2), SLOT_ALIGN_LOG2), cap - win)
    a1 = jnp.minimum(a0 + win, cap - win)
    return pl.multiple_of(a0, 1 << SLOT_ALIGN_LOG2), pl.multiple_of(a1, 1 << SLOT_ALIGN_LOG2)


def _column_halves(d):
    return (slice(0, d // 2), slice(d // 2, d))


def _slot_window_size(cap):
    win = min(SLOT_WINDOW, cap)
    assert cap % (1 << SLOT_ALIGN_LOG2) == 0 and (win == cap or 2 * win >= TOKEN_BLOCK + (1 << SLOT_ALIGN_LOG2))
    return win


def _gather_kernel(cnt_ref, slot_ref, h2_ref, xg_ref, acc_ref):
    b, e = pl.program_id(0), pl.program_id(1)
    base = (b * pl.num_programs(1) + e) * CNT_PAD
    cap = xg_ref.shape[0]
    S = h2_ref.shape[0]
    win = _slot_window_size(cap)
    acc_ref[...] = jnp.zeros_like(acc_ref)

    def scatter_rows(c, a, lo):
        srow = slot_ref[:, c * TOKEN_BLOCK:(c + 1) * TOKEN_BLOCK]
        htile = h2_ref[c * TOKEN_BLOCK:(c + 1) * TOKEN_BLOCK, :]
        p = lax.broadcasted_iota(jnp.int32, (win, TOKEN_BLOCK), 0) + a
        onehot = jnp.where((p == srow) & (p >= lo), 1.0, 0.0).astype(BF16)
        acc_ref[pl.ds(a, win), :] += jnp.dot(onehot, htile, preferred_element_type=F32)

    for c in range(S // TOKEN_BLOCK):
        a0, _ = _slot_windows(cnt_ref[base + c], cap, win)
        scatter_rows(c, a0, 0)
    for c in range(S // TOKEN_BLOCK):
        a0, a1 = _slot_windows(cnt_ref[base + c], cap, win)

        @pl.when(cnt_ref[base + c + 1] > a0 + win)
        def _():
            scatter_rows(c, a1, a0 + win)
    xg_ref[...] = acc_ref[...].astype(BF16)


def _gather(cnt_flat, slot, h2, cap):
    B, E, S = slot.shape
    D = h2.shape[-1]
    return pl.pallas_call(
        _gather_kernel,
        grid_spec=pltpu.PrefetchScalarGridSpec(
            num_scalar_prefetch=1,
            grid=(B, E),
            in_specs=[pl.BlockSpec((None, None, 1, S), lambda b, e, cnt: (b, e, 0, 0)),
                      pl.BlockSpec((None, S, D), lambda b, e, cnt: (b, 0, 0))],
            out_specs=pl.BlockSpec((None, None, cap, D), lambda b, e, cnt: (b, e, 0, 0)),
            scratch_shapes=[pltpu.VMEM((cap, D), F32)]),
        out_shape=jax.ShapeDtypeStruct((B, E, cap, D), BF16),
        compiler_params=_cparams("parallel", "arbitrary"),
        name="gather",
    )(cnt_flat, slot.reshape(B, E, 1, S), h2)


def _experts_kernel(xg_ref, wg_ref, wu_ref, wd_ref, y_ref, wgb, wub, wdb):
    @pl.when(pl.program_id(1) == 0)
    def _():
        wgb[...] = wg_ref[...].astype(BF16)
        wub[...] = wu_ref[...].astype(BF16)
        wdb[...] = wd_ref[...].astype(BF16)

    xg = xg_ref[...]
    g = jnp.dot(xg, wgb[...], preferred_element_type=F32)
    u = jnp.dot(xg, wub[...], preferred_element_type=F32)
    a = (g * jax.nn.sigmoid(g) * u).astype(BF16)
    y_ref[...] = jnp.dot(a, wdb[...], preferred_element_type=F32).astype(BF16)


def _experts(xg, w_gate, w_up, w_down, layer):
    B, E, cap, D = xg.shape
    F = w_gate.shape[-1]
    tok = pl.BlockSpec((None, None, cap, D), lambda e, b: (b, e, 0, 0))
    return pl.pallas_call(
        _experts_kernel,
        grid=(E, B),
        in_specs=[tok,
                  pl.BlockSpec((None, None, D, F), lambda e, b: (layer, e, 0, 0)),
                  pl.BlockSpec((None, None, D, F), lambda e, b: (layer, e, 0, 0)),
                  pl.BlockSpec((None, None, F, D), lambda e, b: (layer, e, 0, 0))],
        out_specs=tok,
        out_shape=jax.ShapeDtypeStruct((B, E, cap, D), BF16),
        scratch_shapes=[pltpu.VMEM((D, F), BF16), pltpu.VMEM((D, F), BF16), pltpu.VMEM((F, D), BF16)],
        compiler_params=_cparams("arbitrary", "arbitrary"),
        name="experts",
    )(xg, w_gate, w_up, w_down)


def _combine_kernel(cnt_ref, slot_ref, wts_ref, y_ref, x_ref, mod_ref, fg_ref, o_ref, acc_ref, *, final_norm):
    b, c = pl.program_id(0), pl.program_id(1)
    E, cap, D = y_ref.shape
    win = _slot_window_size(cap)
    acc_ref[...] = jnp.zeros_like(acc_ref)

    def spread_rows(e, a, lo):
        scol = slot_ref[:, e:e + 1]
        wcol = wts_ref[:, e:e + 1]
        lane = lax.broadcasted_iota(jnp.int32, (TOKEN_BLOCK, win), 1) + a
        onehot_t = jnp.where((lane == scol) & (lane >= lo), 1.0, 0.0).astype(BF16)
        acc_ref[...] += wcol * jnp.dot(onehot_t, y_ref[e, pl.ds(a, win), :], preferred_element_type=F32)

    for e in range(E):
        a0, _ = _slot_windows(cnt_ref[(b * E + e) * CNT_PAD + c], cap, win)
        spread_rows(e, a0, 0)
    for e in range(E):
        base = (b * E + e) * CNT_PAD + c
        a0, a1 = _slot_windows(cnt_ref[base], cap, win)

        @pl.when(cnt_ref[base + 1] > a0 + win)
        def _():
            spread_rows(e, a1, a0 + win)
    out = x_ref[...] + mod_ref[5:6, :] * acc_ref[...]
    if final_norm:
        out = _rms(out, fg_ref[...])
    o_ref[...] = out


def _combine(cnt_flat, slot_t, wts_t, y, x, mod_l, final_g, final_norm):
    B, S, D = x.shape
    E, cap = y.shape[1], y.shape[2]
    tok = lambda b, c, cnt: (b, c, 0)
    return pl.pallas_call(
        functools.partial(_combine_kernel, final_norm=final_norm),
        grid_spec=pltpu.PrefetchScalarGridSpec(
            num_scalar_prefetch=1,
            grid=(B, S // TOKEN_BLOCK),
            in_specs=[pl.BlockSpec((None, TOKEN_BLOCK, E), tok),
                      pl.BlockSpec((None, TOKEN_BLOCK, E), tok),
                      pl.BlockSpec((None, E, cap, D), lambda b, c, cnt: (b, 0, 0, 0)),
                      pl.BlockSpec((None, TOKEN_BLOCK, D), tok),
                      pl.BlockSpec((None, 6, D), lambda b, c, cnt: (b, 0, 0)),
                      pl.BlockSpec((1, D), lambda b, c, cnt: (0, 0))],
            out_specs=pl.BlockSpec((None, TOKEN_BLOCK, D), tok),
            scratch_shapes=[pltpu.VMEM((TOKEN_BLOCK, D), F32)]),
        out_shape=jax.ShapeDtypeStruct((B, S, D), F32),
        compiler_params=_cparams("parallel", "arbitrary"),
        name="combine",
    )(cnt_flat, slot_t, wts_t, y, x, mod_l, final_g)


def _prep_layer(l, w_in, w_uq, w_ukv, w_oa, w_pool, w_oc, w_out, w_router, width):
    D = w_in.shape[1]
    half = QK_ROPE // 2
    o_ckv = Q_RANK
    o_kr = o_ckv + KV_RANK
    o_pu = o_kr + QK_ROPE
    o_gl = o_pu + 4 * width
    wi = w_in[l]
    kr = wi[:, o_kr:o_pu]
    z64 = jnp.zeros((D, QK_NOPE), F32)
    z32 = jnp.zeros((D, HEAD_PAD - QK_NOPE - QK_ROPE), F32)
    kr_a = jnp.concatenate([z64, kr, z32], axis=1)
    kr_b = jnp.concatenate([z64, kr[:, half:], kr[:, :half], z32], axis=1)
    w1 = jnp.concatenate([wi[:, :o_kr], kr_a, kr_b, wi[:, o_pu:o_gl]], axis=1).astype(BF16)
    wg = wi[:, o_gl:].astype(BF16)

    dqk = QK_NOPE + QK_ROPE
    q3 = w_uq[l].reshape(Q_RANK, N_HEADS, dqk)
    zq = jnp.zeros((Q_RANK, N_HEADS, HEAD_PAD - dqk), F32)
    wqa = jnp.concatenate([q3, zq], axis=-1).reshape(Q_RANK, N_HEADS * HEAD_PAD).astype(BF16)
    wqb = jnp.concatenate([jnp.zeros((Q_RANK, N_HEADS, QK_NOPE), F32), q3[..., QK_NOPE + half:],
                           q3[..., QK_NOPE:QK_NOPE + half], zq], axis=-1)
    wqb = wqb.reshape(Q_RANK, N_HEADS * HEAD_PAD).astype(BF16)

    kv3 = w_ukv[l].reshape(KV_RANK, N_HEADS, QK_NOPE + V_HEAD)
    wk = jnp.concatenate([kv3[..., :QK_NOPE], jnp.zeros((KV_RANK, N_HEADS, HEAD_PAD - QK_NOPE), F32)], axis=-1)
    wkv = jnp.concatenate([wk.reshape(KV_RANK, N_HEADS * HEAD_PAD),
                           kv3[..., QK_NOPE:].reshape(KV_RANK, N_HEADS * V_HEAD)], axis=1).astype(BF16)
    return dict(w1=w1, wg=wg, wqa=wqa, wqb=wqb, wkv=wkv,
                woa=w_oa[l].astype(BF16), wpool=w_pool[l].astype(BF16), woc=w_oc[l].astype(BF16),
                wout=w_out[l].astype(BF16), wrt=w_router[l].T)


def kernel(x, c, positions, w_mod, b_mod, norm1_g, w_in, b_gate, q_norm_g, w_uq, kv_norm_g, w_ukv, w_oa, w_pool,
           pool_scale, conv_w, w_oc, w_out, norm2_g, w_router, w_gate, w_up, w_down, final_g):
    B, S, D = x.shape
    L = w_mod.shape[0]
    E = w_router.shape[-1]
    width = w_oc.shape[1]
    cap = EC_FACTOR * S // E
    assert S % TOKEN_BLOCK == 0 and S // TOKEN_BLOCK + 1 <= CNT_PAD
    tm = min(S, 512)
    tq = min(S, 256)

    mod = _modulation(c, w_mod, b_mod).reshape(L, B, 6, D)
    ck, sk = _rope_tables(positions)
    row = lambda a: a.reshape(1, -1)
    for l in range(L):
        w = _prep_layer(l, w_in, w_uq, w_ukv, w_oa, w_pool, w_oc, w_out, w_router, width)
        q, k, v, pu, z, cb = _attn_in(x, mod[l], row(norm1_g[l]), w["w1"], row(q_norm_g[l]), w["wqa"], w["wqb"],
                                      row(kv_norm_g[l]), w["wkv"], ck, sk, tm)
        o = _attention(q, k, v, tq)
        x, h2, logits_t = _mix_post(x, o, pu, z, cb, mod[l], row(norm1_g[l]), w["wg"], row(b_gate[l]), w["woa"],
                                    w["wpool"], row(pool_scale[l]), conv_w[l], w["woc"], w["wout"],
                                    row(norm2_g[l]), w["wrt"], tm)
        slot, wts, cnt = _route(logits_t, cap)
        cnt_flat = cnt[:, :, :CNT_PAD].reshape(-1)
        xg = _gather(cnt_flat, slot, h2, cap)
        y = _experts(xg, w_gate, w_up, w_down, l)
        x = _combine(cnt_flat, slot.transpose(0, 2, 1), wts.transpose(0, 2, 1), y, x, mod[l], row(final_g),
                     final_norm=(l == L - 1))
    return x
```

```python
import functools

import jax
import jax.numpy as jnp
import numpy as np
from jax import lax
from jax.experimental import pallas as pl
from jax.experimental.pallas import tpu as pltpu

N_HEADS = 8
QK_NOPE = 64
QK_ROPE = 32
V_HEAD = 64
Q_RANK = 384
KV_RANK = 256
ROPE_THETA = 10000.0
POOL_WINDOWS = (2, 4, 8, 16)
POOL_GROUP = 128
CONV_K = 3
N_BRANCH = 3
N_EXPERTS = 16
EC_FACTOR = 2
EPS = 1e-6

LANES = 128
HEAD_PAD = 128
HALO = 16
SLOT_WINDOW = 64
SLOT_ALIGN_LOG2 = 4
TOKEN_BLOCK = 256
CNT_PAD = 32
VMEM_LIMIT = 56 * 1024 * 1024

BF16 = jnp.bfloat16
F32 = jnp.float32


def _cparams(*sem):
    return pltpu.CompilerParams(dimension_semantics=sem, vmem_limit_bytes=VMEM_LIMIT)


def _rms(x, g):
    return x * lax.rsqrt(jnp.mean(x * x, axis=-1, keepdims=True) + EPS) * g


def _mod_kernel(c_ref, w_ref, b_ref, o_ref):
    c = c_ref[...]
    c_act = c * jax.nn.sigmoid(c)
    o_ref[...] = jnp.dot(c_act, w_ref[...], preferred_element_type=F32,
                         precision=lax.Precision.HIGHEST) + b_ref[...]


def _modulation(c, w_mod, b_mod):
    L, D, N = w_mod.shape
    B = c.shape[0]
    tn = 1536
    return pl.pallas_call(
        _mod_kernel,
        grid=(L, N // tn),
        in_specs=[pl.BlockSpec((B, D), lambda l, j: (0, 0)),
                  pl.BlockSpec((None, D, tn), lambda l, j: (l, 0, j)),
                  pl.BlockSpec((None, 1, tn), lambda l, j: (l, 0, j))],
        out_specs=pl.BlockSpec((None, B, tn), lambda l, j: (l, 0, j)),
        out_shape=jax.ShapeDtypeStruct((L, B, N), F32),
        compiler_params=_cparams("parallel", "parallel"),
        name="modulation",
    )(c, w_mod, b_mod.reshape(L, 1, N))


def _rope_kernel(pos_ref, freq_ref, ck_ref, sk_ref):
    ang = pos_ref[...].astype(F32) * freq_ref[...]
    lane = lax.broadcasted_iota(jnp.int32, ang.shape, 1)
    cosv, sinv = jnp.cos(ang), jnp.sin(ang)
    half = QK_ROPE // 2
    in_rope = (lane >= QK_NOPE) & (lane < QK_NOPE + QK_ROPE)
    ck_ref[...] = jnp.where(in_rope, cosv, 0.0)
    sk_ref[...] = jnp.where(in_rope, jnp.where(lane < QK_NOPE + half, -sinv, sinv), 0.0)


def _rope_tables(positions):
    B, S = positions.shape
    half = QK_ROPE // 2
    freqs = ROPE_THETA ** (-jnp.arange(0, QK_ROPE, 2, dtype=F32) / QK_ROPE)
    freq_row = jnp.concatenate([jnp.zeros((QK_NOPE,), F32), freqs, freqs,
                                jnp.zeros((HEAD_PAD - QK_NOPE - 2 * half,), F32)]).reshape(1, HEAD_PAD)
    ts = min(S, 1024)
    out = jax.ShapeDtypeStruct((B, S, HEAD_PAD), F32)
    return pl.pallas_call(
        _rope_kernel,
        grid=(B, S // ts),
        in_specs=[pl.BlockSpec((None, ts, 1), lambda b, i: (b, i, 0)),
                  pl.BlockSpec((1, HEAD_PAD), lambda b, i: (0, 0))],
        out_specs=[pl.BlockSpec((None, ts, HEAD_PAD), lambda b, i: (b, i, 0))] * 2,
        out_shape=[out, out],
        compiler_params=_cparams("parallel", "parallel"),
        name="rope_tables",
    )(positions.reshape(B, S, 1), freq_row)


def _rope_partner(x, lane_in_head):
    half = QK_ROPE // 2
    n = x.shape[1]
    from_upper = pltpu.roll(x, n - half, axis=1)
    from_lower = pltpu.roll(x, half, axis=1)
    return jnp.where(lane_in_head < QK_NOPE + half, from_upper, from_lower)


def _attn_in_kernel(x_ref, mod_ref, n1g_ref, w1_ref, qng_ref, wq_ref, kvng_ref, wkv_ref,
                    ck_ref, sk_ref,
                    q_ref, k_ref, v_ref, pu_ref, z_ref, cb_ref):
    x = x_ref[...]
    sh1, sc1 = mod_ref[0:1, :], mod_ref[1:2, :]
    h = _rms(x, n1g_ref[...]) * (1.0 + sc1) + sh1
    p = jnp.dot(h.astype(BF16), w1_ref[...], preferred_element_type=F32)
    o_ckv = Q_RANK
    o_kr = o_ckv + KV_RANK
    o_pu = o_kr + HEAD_PAD
    width = pu_ref.shape[-1]
    o_cx, o_cb, o_cc = o_pu + width, o_pu + 2 * width, o_pu + 3 * width
    cqn = _rms(p[:, 0:o_ckv], qng_ref[...]).astype(BF16)
    ckvn = _rms(p[:, o_ckv:o_kr], kvng_ref[...]).astype(BF16)
    q = jnp.dot(cqn, wq_ref[...], preferred_element_type=F32)
    kvp = jnp.dot(ckvn, wkv_ref[...], preferred_element_type=F32)
    ck, sk = ck_ref[...], sk_ref[...]
    lane = lax.broadcasted_iota(jnp.int32, ck.shape, 1)
    cq_tab = jnp.where(lane < QK_NOPE, 1.0, ck)
    scale = (QK_NOPE + QK_ROPE) ** -0.5 * float(np.log2(np.e))
    kr = p[:, o_kr:o_pu]
    krope = kr * ck + _rope_partner(kr, lane) * sk
    q_swap = _rope_partner(q, jnp.tile(lane, (1, N_HEADS)))
    for hd in range(N_HEADS):
        sl = slice(hd * HEAD_PAD, (hd + 1) * HEAD_PAD)
        q_ref[hd] = ((q[:, sl] * cq_tab + q_swap[:, sl] * sk) * scale).astype(BF16)
        k_ref[hd] = (kvp[:, sl] + krope).astype(BF16)
    v_ref[...] = kvp[:, N_HEADS * HEAD_PAD:].astype(BF16)
    pu_ref[...] = p[:, o_pu:o_cx]
    z_ref[...] = p[:, o_cc:o_cc + width] * p[:, o_cx:o_cb]
    cb_ref[...] = p[:, o_cb:o_cc]


def _attn_in(x, mod_l, n1g, w1, qng, wq, kvng, wkv, ck, sk, tm):
    B, S, D = x.shape
    width = (w1.shape[1] - Q_RANK - KV_RANK - HEAD_PAD) // 4
    tok = lambda b, i: (b, i, 0)
    full = lambda b, i: (0, 0)
    hspec = pl.BlockSpec((None, N_HEADS, tm, HEAD_PAD), lambda b, i: (b, 0, i, 0))
    wspec = lambda a: pl.BlockSpec(a.shape, full)
    f32o = jax.ShapeDtypeStruct((B, S, width), F32)
    return pl.pallas_call(
        _attn_in_kernel,
        grid=(B, S // tm),
        in_specs=[pl.BlockSpec((None, tm, D), tok),
                  pl.BlockSpec((None, 6, D), lambda b, i: (b, 0, 0)),
                  wspec(n1g), wspec(w1), wspec(qng), wspec(wq), wspec(kvng), wspec(wkv),
                  pl.BlockSpec((None, tm, HEAD_PAD), tok),
                  pl.BlockSpec((None, tm, HEAD_PAD), tok)],
        out_specs=[hspec, hspec,
                   pl.BlockSpec((None, tm, N_HEADS * V_HEAD), tok),
                   pl.BlockSpec((None, tm, width), tok),
                   pl.BlockSpec((None, tm, width), tok),
                   pl.BlockSpec((None, tm, width), tok)],
        out_shape=[jax.ShapeDtypeStruct((B, N_HEADS, S, HEAD_PAD), BF16),
                   jax.ShapeDtypeStruct((B, N_HEADS, S, HEAD_PAD), BF16),
                   jax.ShapeDtypeStruct((B, S, N_HEADS * V_HEAD), BF16),
                   f32o, f32o, f32o],
        compiler_params=_cparams("parallel", "parallel"),
        name="attn_in",
    )(x, mod_l, n1g, w1, qng, wq, kvng, wkv, ck, sk)


def _attention_kernel(q_ref, k_ref, v_ref, o_ref):
    nh = q_ref.shape[0]
    pair = 2 * V_HEAD
    scores = [lax.dot_general(q_ref[hd], k_ref[hd], (((1,), (1,)), ((), ())),
                              preferred_element_type=F32) for hd in range(nh)]
    outs = []
    for hd in range(nh):
        s = scores[hd]
        m = jnp.max(s, axis=-1, keepdims=True)
        p = jnp.exp2(s - m)
        l = jnp.sum(p, axis=-1, keepdims=True)
        v = v_ref[:, (hd // 2) * pair:(hd // 2 + 1) * pair]
        o = jnp.dot(p.astype(BF16), v, preferred_element_type=F32)
        outs.append(o / l)
    lane = lax.broadcasted_iota(jnp.int32, outs[0].shape, 1)
    for pr in range(nh // 2):
        o_ref[:, pr * pair:(pr + 1) * pair] = jnp.where(lane < V_HEAD, outs[2 * pr], outs[2 * pr + 1]).astype(BF16)


def _attention(q, k, v, tq, nh=4):
    B, H, S, _ = q.shape
    return pl.pallas_call(
        _attention_kernel,
        grid=(B, H // nh, S // tq),
        in_specs=[pl.BlockSpec((None, nh, tq, HEAD_PAD), lambda b, h, i: (b, h, i, 0)),
                  pl.BlockSpec((None, nh, S, HEAD_PAD), lambda b, h, i: (b, h, 0, 0)),
                  pl.BlockSpec((None, S, nh * V_HEAD), lambda b, h, i: (b, 0, h))],
        out_specs=pl.BlockSpec((None, tq, nh * V_HEAD), lambda b, h, i: (b, i, h)),
        out_shape=jax.ShapeDtypeStruct((B, S, H * V_HEAD), BF16),
        compiler_params=_cparams("parallel", "parallel", "parallel"),
        name="attention",
    )(q, k, v)


def _shift_rows(ext, shift, ts):
    n = ext.shape[0]
    return pltpu.roll(ext, shift % n, axis=0)[HALO:HALO + ts]


def _mix_post_kernel(x_ref, o_ref, pu_ref, pup_ref, pun_ref, z_ref, zp_ref, zn_ref, cb_ref,
                     mod_ref, n1g_ref, wg_ref, bg_ref, woa_ref, wpool_ref, pscale_ref, convw_ref, woc_ref,
                     wout_ref, n2g_ref, wrt_ref,
                     xo_ref, h2_ref, lg_ref, *, seq_len, sub):
    i = pl.program_id(1)
    last = pl.num_programs(1) - 1
    ts, D = x_ref.shape
    sh1, sc1, g1 = mod_ref[0:1, :], mod_ref[1:2, :], mod_ref[2:3, :]
    sh2, sc2 = mod_ref[3:4, :], mod_ref[4:5, :]
    first_f = jnp.where(i == 0, 0.0, 1.0)
    last_f = jnp.where(i == last, 0.0, 1.0)
    pext = jnp.concatenate([pup_ref[...] * first_f, pu_ref[...], pun_ref[...] * last_f], axis=0)
    zext = jnp.concatenate([zp_ref[...] * first_f, z_ref[...], zn_ref[...] * last_f], axis=0)
    wr = wrt_ref[...]
    wr_hi = wr.astype(BF16)
    wr_hl = jnp.concatenate([wr_hi, (wr - wr_hi.astype(F32)).astype(BF16)], axis=0)
    n_e = wr.shape[0]
    nt_dims = (((1,), (1,)), ((), ()))

    for r0 in range(0, ts, sub):
        rows = slice(r0, r0 + sub)
        x = x_ref[rows, :]
        h = (_rms(x, n1g_ref[...]) * (1.0 + sc1) + sh1).astype(BF16)
        gl = jnp.dot(h, wg_ref[...], preferred_element_type=F32) + bg_ref[...]
        gates = jax.nn.sigmoid(gl)
        ya = jnp.dot(o_ref[rows, :], woa_ref[...], preferred_element_type=F32)

        t = i * ts + r0 + lax.broadcasted_iota(jnp.int32, (sub, 1), 0)
        pe = pext[r0:r0 + sub + 2 * HALO]
        yb_parts = []
        for gi, w in enumerate(POOL_WINDOWS):
            gs = slice(gi * POOL_GROUP, (gi + 1) * POOL_GROUP)
            a = pe[:, gs]
            span = 1
            while span < w:
                a = a + pltpu.roll(a, span, axis=0)
                span *= 2
            wsum = _shift_rows(a, -(w // 2 - 1), sub)
            lo = jnp.maximum(t - w // 2, 0)
            hi = jnp.minimum(t + w // 2 - 1, seq_len - 1)
            cnt = (hi - lo + 1).astype(F32)
            mixed = wsum / cnt - pe[HALO:HALO + sub, gs]
            yb_parts.append(jnp.dot(mixed.astype(BF16), wpool_ref[gi], preferred_element_type=F32))
        yb = jnp.concatenate(yb_parts, axis=-1) * pscale_ref[...]

        ze = zext[r0:r0 + sub + 2 * HALO]
        yconv = (convw_ref[0:1, :] * _shift_rows(ze, 1, sub) + convw_ref[1:2, :] * ze[HALO:HALO + sub]
                 + convw_ref[2:3, :] * _shift_rows(ze, -1, sub))
        yc = jnp.dot((cb_ref[rows, :] * yconv).astype(BF16), woc_ref[...], preferred_element_type=F32)

        merged = gates[:, 0:D] * ya + gates[:, D:2 * D] * yb + gates[:, 2 * D:3 * D] * yc
        xn = x + g1 * jnp.dot(merged.astype(BF16), wout_ref[...], preferred_element_type=F32)
        xo_ref[rows, :] = xn
        h2 = _rms(xn, n2g_ref[...]) * (1.0 + sc2) + sh2
        h2_hi = h2.astype(BF16)
        h2_ref[rows, :] = h2_hi
        h2_lo = (h2 - h2_hi.astype(F32)).astype(BF16)
        r_hi = lax.dot_general(wr_hl, h2_hi, nt_dims, preferred_element_type=F32)
        r_lo = lax.dot_general(wr_hi, h2_lo, nt_dims, preferred_element_type=F32)
        lg_ref[:, rows] = r_hi[:n_e] + r_hi[n_e:] + r_lo


def _mix_post(x, o, pu, z, cb, mod_l, n1g, wg, bg, woa, wpool, pscale, convw, woc, wout, n2g, wrt, ts):
    B, S, D = x.shape
    W = pu.shape[-1]
    E = wrt.shape[0]
    nh = ts // HALO
    tok = lambda b, i: (b, i, 0)
    prev = lambda b, i: (b, jnp.maximum(i * nh - 1, 0), 0)
    nxt = lambda b, i: (b, jnp.minimum((i + 1) * nh, S // HALO - 1), 0)
    wspec = lambda a: pl.BlockSpec(a.shape, lambda b, i: (0,) * a.ndim)
    return pl.pallas_call(
        functools.partial(_mix_post_kernel, seq_len=S, sub=min(ts, 256)),
        grid=(B, S // ts),
        in_specs=[pl.BlockSpec((None, ts, D), tok),
                  pl.BlockSpec((None, ts, o.shape[-1]), tok),
                  pl.BlockSpec((None, ts, W), tok),
                  pl.BlockSpec((None, HALO, W), prev),
                  pl.BlockSpec((None, HALO, W), nxt),
                  pl.BlockSpec((None, ts, W), tok),
                  pl.BlockSpec((None, HALO, W), prev),
                  pl.BlockSpec((None, HALO, W), nxt),
                  pl.BlockSpec((None, ts, W), tok),
                  pl.BlockSpec((None, 6, D), lambda b, i: (b, 0, 0)),
                  wspec(n1g), wspec(wg), wspec(bg), wspec(woa), wspec(wpool), wspec(pscale), wspec(convw),
                  wspec(woc), wspec(wout), wspec(n2g), wspec(wrt)],
        out_specs=[pl.BlockSpec((None, ts, D), tok),
                   pl.BlockSpec((None, ts, D), tok),
                   pl.BlockSpec((None, E, ts), lambda b, i: (b, 0, i))],
        out_shape=[jax.ShapeDtypeStruct((B, S, D), F32),
                   jax.ShapeDtypeStruct((B, S, D), BF16),
                   jax.ShapeDtypeStruct((B, E, S), F32)],
        compiler_params=_cparams("parallel", "parallel"),
        name="mix_post",
    )(x, o, pu, pu, pu, z, z, z, cb, mod_l, n1g, wg, bg, woa, wpool, pscale, convw, woc, wout, n2g, wrt)


def _prefix_chunks(flags, tri):
    E, S = flags.shape
    running = jnp.zeros((E, 1), F32)
    pieces, starts = [], []
    for c in range(S // TOKEN_BLOCK):
        blk = flags[:, c * TOKEN_BLOCK:(c + 1) * TOKEN_BLOCK]
        incl = jnp.dot(blk.astype(BF16), tri, preferred_element_type=F32)
        pieces.append(incl - blk + running)
        starts.append(running)
        running = running + jnp.sum(blk, axis=-1, keepdims=True)
    starts.append(running)
    return jnp.concatenate(pieces, axis=-1), starts


def _route_kernel(lg_ref, slot_ref, wts_ref, cnt_ref, *, cap):
    lg = lg_ref[...]
    E, S = lg.shape
    m = jnp.max(lg, axis=0, keepdims=True)
    ex = jnp.exp(lg - m)
    aff = ex / jnp.sum(ex, axis=0, keepdims=True)

    def step(it, tau_bits):
        cand = tau_bits | jnp.left_shift(jnp.int32(1), 30 - it)
        n = jnp.sum((aff >= lax.bitcast_convert_type(cand, F32)).astype(jnp.int32), axis=-1, keepdims=True)
        return jnp.where(n >= cap, cand, tau_bits)

    tau = lax.bitcast_convert_type(lax.fori_loop(0, 31, step, jnp.zeros((E, 1), jnp.int32)), F32)
    gt = aff > tau
    eq = aff == tau
    need = cap - jnp.sum(gt.astype(jnp.int32), axis=-1, keepdims=True)
    row = lax.broadcasted_iota(jnp.int32, (TOKEN_BLOCK, TOKEN_BLOCK), 0)
    col = lax.broadcasted_iota(jnp.int32, (TOKEN_BLOCK, TOKEN_BLOCK), 1)
    tri = jnp.where(row <= col, 1.0, 0.0).astype(BF16)
    eq_rank, _ = _prefix_chunks(jnp.where(eq, 1.0, 0.0), tri)
    sel = gt | (eq & (eq_rank.astype(jnp.int32) < need))
    pos, starts = _prefix_chunks(jnp.where(sel, 1.0, 0.0), tri)
    slot_ref[...] = jnp.where(sel, pos.astype(jnp.int32), -1)
    wts_ref[...] = jnp.where(sel, aff, 0.0)
    lane = lax.broadcasted_iota(jnp.int32, (E, LANES), 1)
    cnt = jnp.full((E, LANES), cap, jnp.int32)
    for c, st in enumerate(starts):
        cnt = jnp.where(lane == c, st.astype(jnp.int32), cnt)
    cnt_ref[...] = cnt


def _route(logits_t, cap):
    B, E, S = logits_t.shape
    spec = pl.BlockSpec((None, E, S), lambda b: (b, 0, 0))
    return pl.pallas_call(
        functools.partial(_route_kernel, cap=cap),
        grid=(B,),
        in_specs=[spec],
        out_specs=[spec, spec, pl.BlockSpec((None, E, LANES), lambda b: (b, 0, 0))],
        out_shape=[jax.ShapeDtypeStruct((B, E, S), jnp.int32),
                   jax.ShapeDtypeStruct((B, E, S), F32),
                   jax.ShapeDtypeStruct((B, E, LANES), jnp.int32)],
        compiler_params=_cparams("parallel"),
        name="route",
    )(logits_t)


def _window_start(c0, cap):
    a0 = jnp.left_shift(jnp.right_shift(c0, SLOT_ALIGN_LOG2), SLOT_ALIGN_LOG2)
    return pl.multiple_of(jnp.minimum(a0, cap - SLOT_WINDOW), 1 << SLOT_ALIGN_LOG2)


def _gather_kernel(cnt_ref, slot_ref, wts_ref, h2_ref, xg_ref, tv_ref):
    b, c = pl.program_id(0), pl.program_id(1)
    E, cap, D = xg_ref.shape
    win = SLOT_WINDOW

    @pl.when(c == 0)
    def _():
        xg_ref[...] = jnp.zeros(xg_ref.shape, BF16)
        tv_ref[...] = jnp.zeros(tv_ref.shape, F32)

    htile = h2_ref[...]
    p_iota = lax.broadcasted_iota(jnp.int32, (win, TOKEN_BLOCK), 0)
    first = [cnt_ref[(b * E + e) * CNT_PAD + c] for e in range(E)]
    starts = [_window_start(first[e], cap) for e in range(E)]

    def add_rows(e, a, hit, rows):
        sl = pl.ds(a, win)
        xg_ref[e, sl, :] = (xg_ref[e, sl, :].astype(F32) + rows).astype(BF16)
        tv_ref[e, sl, :] += jnp.sum(jnp.where(hit, wts_ref[e:e + 1, :], 0.0), axis=1, keepdims=True)

    hits = [(p_iota + starts[e]) == slot_ref[e:e + 1, :] for e in range(E)]
    onehot = jnp.concatenate([jnp.where(h, 1.0, 0.0).astype(BF16) for h in hits], axis=0)
    rows = jnp.dot(onehot, htile, preferred_element_type=F32)
    for e in range(E):
        add_rows(e, starts[e], hits[e], rows[e * win:(e + 1) * win])

    for e in range(E):
        last = cnt_ref[(b * E + e) * CNT_PAD + c + 1]

        @pl.when(last > starts[e] + win)
        def _():
            def more(k, carry):
                lo = starts[e] + k * win
                a = pl.multiple_of(jnp.minimum(lo, cap - win), 1 << SLOT_ALIGN_LOG2)
                hit = ((p_iota + a) == slot_ref[e:e + 1, :]) & ((p_iota + a) >= lo)
                add_rows(e, a, hit, jnp.dot(jnp.where(hit, 1.0, 0.0).astype(BF16), htile,
                                            preferred_element_type=F32))
                return carry

            lax.fori_loop(1, (last - starts[e] + win - 1) // win, more, 0)


def _gather(cnt_flat, slot, wts, h2, cap):
    B, E, S = slot.shape
    D = h2.shape[-1]
    assert cap % SLOT_WINDOW == 0
    tile = lambda b, c, cnt: (b, 0, c)
    return pl.pallas_call(
        _gather_kernel,
        grid_spec=pltpu.PrefetchScalarGridSpec(
            num_scalar_prefetch=1,
            grid=(B, S // TOKEN_BLOCK),
            in_specs=[pl.BlockSpec((None, E, TOKEN_BLOCK), tile),
                      pl.BlockSpec((None, E, TOKEN_BLOCK), tile),
                      pl.BlockSpec((None, TOKEN_BLOCK, D), lambda b, c, cnt: (b, c, 0))],
            out_specs=[pl.BlockSpec((None, E, cap, D), lambda b, c, cnt: (b, 0, 0, 0)),
                       pl.BlockSpec((None, E, cap, 1), lambda b, c, cnt: (b, 0, 0, 0))]),
        out_shape=[jax.ShapeDtypeStruct((B, E, cap, D), BF16),
                   jax.ShapeDtypeStruct((B, E, cap, 1), F32)],
        compiler_params=_cparams("parallel", "arbitrary"),
        name="gather",
    )(cnt_flat, slot, wts, h2)


def _experts_kernel(xg_ref, tv_ref, wg_ref, wu_ref, wd_ref, y_ref, wgb, wub, wdb):
    @pl.when(pl.program_id(1) == 0)
    def _():
        wgb[...] = wg_ref[...].astype(BF16)
        wub[...] = wu_ref[...].astype(BF16)
        wdb[...] = wd_ref[...].astype(BF16)

    xg = xg_ref[...]
    g = jnp.dot(xg, wgb[...], preferred_element_type=F32)
    u = jnp.dot(xg, wub[...], preferred_element_type=F32)
    a = (g * jax.nn.sigmoid(g) * u).astype(BF16)
    y_ref[...] = (jnp.dot(a, wdb[...], preferred_element_type=F32) * tv_ref[...]).astype(BF16)


def _experts(xg, tv, w_gate, w_up, w_down, layer):
    B, E, cap, D = xg.shape
    F = w_gate.shape[-1]
    tok = pl.BlockSpec((None, None, cap, D), lambda e, b: (b, e, 0, 0))
    return pl.pallas_call(
        _experts_kernel,
        grid=(E, B),
        in_specs=[tok,
                  pl.BlockSpec((None, None, cap, 1), lambda e, b: (b, e, 0, 0)),
                  pl.BlockSpec((None, None, D, F), lambda e, b: (layer, e, 0, 0)),
                  pl.BlockSpec((None, None, D, F), lambda e, b: (layer, e, 0, 0)),
                  pl.BlockSpec((None, None, F, D), lambda e, b: (layer, e, 0, 0))],
        out_specs=tok,
        out_shape=jax.ShapeDtypeStruct((B, E, cap, D), BF16),
        scratch_shapes=[pltpu.VMEM((D, F), BF16), pltpu.VMEM((D, F), BF16), pltpu.VMEM((F, D), BF16)],
        compiler_params=_cparams("arbitrary", "arbitrary"),
        name="experts",
    )(xg, tv, w_gate, w_up, w_down)


def _combine_kernel(cnt_ref, slot_ref, y_ref, x_ref, mod_ref, fg_ref, o_ref, acc_ref, *, final_norm):
    b, c = pl.program_id(0), pl.program_id(1)
    E, cap, D = y_ref.shape
    win = SLOT_WINDOW
    first = [cnt_ref[(b * E + e) * CNT_PAD + c] for e in range(E)]
    starts = [_window_start(first[e], cap) for e in range(E)]
    ycat = jnp.concatenate([y_ref[e, pl.ds(starts[e], win), :] for e in range(E)], axis=0)
    p_iota = lax.broadcasted_iota(jnp.int32, (win, TOKEN_BLOCK), 0)
    contract_slots = (((0,), (0,)), ((), ()))
    onehot = jnp.concatenate([jnp.where((p_iota + starts[e]) == slot_ref[e:e + 1, :], 1.0, 0.0).astype(BF16)
                              for e in range(E)], axis=0)
    acc_ref[...] = lax.dot_general(onehot, ycat, contract_slots, preferred_element_type=F32)

    for e in range(E):
        last = cnt_ref[(b * E + e) * CNT_PAD + c + 1]

        @pl.when(last > starts[e] + win)
        def _():
            def more(k, carry):
                lo = starts[e] + k * win
                a = pl.multiple_of(jnp.minimum(lo, cap - win), 1 << SLOT_ALIGN_LOG2)
                hit = ((p_iota + a) == slot_ref[e:e + 1, :]) & ((p_iota + a) >= lo)
                acc_ref[...] += lax.dot_general(jnp.where(hit, 1.0, 0.0).astype(BF16), y_ref[e, pl.ds(a, win), :],
                                                contract_slots, preferred_element_type=F32)
                return carry

            lax.fori_loop(1, (last - starts[e] + win - 1) // win, more, 0)

    out = x_ref[...] + mod_ref[5:6, :] * acc_ref[...]
    if final_norm:
        out = _rms(out, fg_ref[...])
    o_ref[...] = out


def _combine(cnt_flat, slot, y, x, mod_l, final_g, final_norm):
    B, S, D = x.shape
    E, cap = y.shape[1], y.shape[2]
    tok = lambda b, c, cnt: (b, c, 0)
    return pl.pallas_call(
        functools.partial(_combine_kernel, final_norm=final_norm),
        grid_spec=pltpu.PrefetchScalarGridSpec(
            num_scalar_prefetch=1,
            grid=(B, S // TOKEN_BLOCK),
            in_specs=[pl.BlockSpec((None, E, TOKEN_BLOCK), lambda b, c, cnt: (b, 0, c)),
                      pl.BlockSpec((None, E, cap, D), lambda b, c, cnt: (b, 0, 0, 0)),
                      pl.BlockSpec((None, TOKEN_BLOCK, D), tok),
                      pl.BlockSpec((None, 6, D), lambda b, c, cnt: (b, 0, 0)),
                      pl.BlockSpec((1, D), lambda b, c, cnt: (0, 0))],
            out_specs=pl.BlockSpec((None, TOKEN_BLOCK, D), tok),
            scratch_shapes=[pltpu.VMEM((TOKEN_BLOCK, D), F32)]),
        out_shape=jax.ShapeDtypeStruct((B, S, D), F32),
        compiler_params=_cparams("parallel", "arbitrary"),
        name="combine",
    )(cnt_flat, slot, y, x, mod_l, final_g)


def _prep_layer(l, w_in, w_uq, w_ukv, w_oa, w_pool, w_oc, w_out, w_router, width):
    D = w_in.shape[1]
    half = QK_ROPE // 2
    o_ckv = Q_RANK
    o_kr = o_ckv + KV_RANK
    o_pu = o_kr + QK_ROPE
    o_gl = o_pu + 4 * width
    wi = w_in[l]
    kr = wi[:, o_kr:o_pu]
    z64 = jnp.zeros((D, QK_NOPE), F32)
    z32 = jnp.zeros((D, HEAD_PAD - QK_NOPE - QK_ROPE), F32)
    kr_pad = jnp.concatenate([z64, kr, z32], axis=1)
    w1 = jnp.concatenate([wi[:, :o_kr], kr_pad, wi[:, o_pu:o_gl]], axis=1).astype(BF16)
    wg = wi[:, o_gl:].astype(BF16)

    dqk = QK_NOPE + QK_ROPE
    q3 = w_uq[l].reshape(Q_RANK, N_HEADS, dqk)
    zq = jnp.zeros((Q_RANK, N_HEADS, HEAD_PAD - dqk), F32)
    wq = jnp.concatenate([q3, zq], axis=-1).reshape(Q_RANK, N_HEADS * HEAD_PAD).astype(BF16)

    kv3 = w_ukv[l].reshape(KV_RANK, N_HEADS, QK_NOPE + V_HEAD)
    wk = jnp.concatenate([kv3[..., :QK_NOPE], jnp.zeros((KV_RANK, N_HEADS, HEAD_PAD - QK_NOPE), F32)], axis=-1)
    wkv = jnp.concatenate([wk.reshape(KV_RANK, N_HEADS * HEAD_PAD),
                           kv3[..., QK_NOPE:].reshape(KV_RANK, N_HEADS * V_HEAD)], axis=1).astype(BF16)
    return dict(w1=w1, wg=wg, wq=wq, wkv=wkv,
                woa=w_oa[l].astype(BF16), wpool=w_pool[l].astype(BF16), woc=w_oc[l].astype(BF16),
                wout=w_out[l].astype(BF16), wrt=w_router[l].T)


def kernel(x, c, positions, w_mod, b_mod, norm1_g, w_in, b_gate, q_norm_g, w_uq, kv_norm_g, w_ukv, w_oa, w_pool,
           pool_scale, conv_w, w_oc, w_out, norm2_g, w_router, w_gate, w_up, w_down, final_g):
    B, S, D = x.shape
    L = w_mod.shape[0]
    E = w_router.shape[-1]
    width = w_oc.shape[1]
    cap = EC_FACTOR * S // E
    assert S % TOKEN_BLOCK == 0 and S // TOKEN_BLOCK + 1 <= CNT_PAD
    tm = min(S, 512)
    tq = min(S, 256)

    mod = _modulation(c, w_mod, b_mod).reshape(L, B, 6, D)
    ck, sk = _rope_tables(positions)
    row = lambda a: a.reshape(1, -1)
    for l in range(L):
        w = _prep_layer(l, w_in, w_uq, w_ukv, w_oa, w_pool, w_oc, w_out, w_router, width)
        q, k, v, pu, z, cb = _attn_in(x, mod[l], row(norm1_g[l]), w["w1"], row(q_norm_g[l]), w["wq"],
                                      row(kv_norm_g[l]), w["wkv"], ck, sk, tm)
        o = _attention(q, k, v, tq)
        x, h2, logits_t = _mix_post(x, o, pu, z, cb, mod[l], row(norm1_g[l]), w["wg"], row(b_gate[l]), w["woa"],
                                    w["wpool"], row(pool_scale[l]), conv_w[l], w["woc"], w["wout"],
                                    row(norm2_g[l]), w["wrt"], tm)
        slot, wts, cnt = _route(logits_t, cap)
        cnt_flat = cnt[:, :, :CNT_PAD].reshape(-1)
        xg, tv = _gather(cnt_flat, slot, wts, h2, cap)
        y = _experts(xg, tv, w_gate, w_up, w_down, l)
        x = _combine(cnt_flat, slot, y, x, mod[l], row(final_g), final_norm=(l == L - 1))
    return x
```

```python
import functools

import jax
import jax.numpy as jnp
import numpy as np
from jax import lax
from jax.experimental import pallas as pl
from jax.experimental.pallas import tpu as pltpu

N_HEADS = 8
QK_NOPE = 64
QK_ROPE = 32
V_HEAD = 64
Q_RANK = 384
KV_RANK = 256
ROPE_THETA = 10000.0
POOL_WINDOWS = (2, 4, 8, 16)
POOL_GROUP = 128
CONV_K = 3
N_BRANCH = 3
N_EXPERTS = 16
EC_FACTOR = 2
EPS = 1e-6

LANES = 128
SUBLANES = 8
HEAD_PAD = 128
HALO = 16
SLOT_WINDOW = 64
SLOT_ALIGN_LOG2 = 4
TOKEN_BLOCK = 256
CNT_PAD = 32
VMEM_LIMIT = 56 * 1024 * 1024

BF16 = jnp.bfloat16
F32 = jnp.float32


def _cparams(*sem):
    return pltpu.CompilerParams(dimension_semantics=sem, vmem_limit_bytes=VMEM_LIMIT)


def _rms(x, g):
    return x * lax.rsqrt(jnp.mean(x * x, axis=-1, keepdims=True) + EPS) * g


def _split_bf16(a):
    hi = a.astype(BF16)
    return hi, (a - hi.astype(F32)).astype(BF16)


def _mod_kernel(c_ref, w_ref, b_ref, o_ref):
    c = c_ref[...]
    c_hi, c_lo = _split_bf16(c * jax.nn.sigmoid(c))
    w_hi, w_lo = _split_bf16(w_ref[...])
    rows = c.shape[0]
    r = jnp.dot(jnp.concatenate([c_hi, c_lo], axis=0), w_hi, preferred_element_type=F32)
    out = r[:rows] + r[rows:] + jnp.dot(c_hi, w_lo, preferred_element_type=F32)
    o_ref[...] = out[:o_ref.shape[0]] + b_ref[...]


def _modulation(c, w_mod, b_mod):
    L, D, N = w_mod.shape
    B = c.shape[0]
    tn = 1536
    rows = -(-B // SUBLANES) * SUBLANES
    c = jnp.pad(c, ((0, rows - B), (0, 0)))
    return pl.pallas_call(
        _mod_kernel,
        grid=(L, N // tn),
        in_specs=[pl.BlockSpec((rows, D), lambda l, j: (0, 0)),
                  pl.BlockSpec((None, D, tn), lambda l, j: (l, 0, j)),
                  pl.BlockSpec((None, 1, tn), lambda l, j: (l, 0, j))],
        out_specs=pl.BlockSpec((None, B, tn), lambda l, j: (l, 0, j)),
        out_shape=jax.ShapeDtypeStruct((L, B, N), F32),
        compiler_params=_cparams("parallel", "parallel"),
        name="modulation",
    )(c, w_mod, b_mod.reshape(L, 1, N))


def _rope_kernel(pos_ref, freq_ref, ck_ref, sk_ref):
    ang = pos_ref[...].astype(F32) * freq_ref[...]
    lane = lax.broadcasted_iota(jnp.int32, ang.shape, 1)
    cosv, sinv = jnp.cos(ang), jnp.sin(ang)
    half = QK_ROPE // 2
    in_rope = (lane >= QK_NOPE) & (lane < QK_NOPE + QK_ROPE)
    ck_ref[...] = jnp.where(in_rope, cosv, 0.0)
    sk_ref[...] = jnp.where(in_rope, jnp.where(lane < QK_NOPE + half, -sinv, sinv), 0.0)


def _rope_tables(positions):
    B, S = positions.shape
    half = QK_ROPE // 2
    freqs = ROPE_THETA ** (-jnp.arange(0, QK_ROPE, 2, dtype=F32) / QK_ROPE)
    freq_row = jnp.concatenate([jnp.zeros((QK_NOPE,), F32), freqs, freqs,
                                jnp.zeros((HEAD_PAD - QK_NOPE - 2 * half,), F32)]).reshape(1, HEAD_PAD)
    ts = min(S, 1024)
    out = jax.ShapeDtypeStruct((B, S, HEAD_PAD), F32)
    return pl.pallas_call(
        _rope_kernel,
        grid=(B, S // ts),
        in_specs=[pl.BlockSpec((None, ts, 1), lambda b, i: (b, i, 0)),
                  pl.BlockSpec((1, HEAD_PAD), lambda b, i: (0, 0))],
        out_specs=[pl.BlockSpec((None, ts, HEAD_PAD), lambda b, i: (b, i, 0))] * 2,
        out_shape=[out, out],
        compiler_params=_cparams("parallel", "parallel"),
        name="rope_tables",
    )(positions.reshape(B, S, 1), freq_row)


def _rope_partner(x, lane_in_head):
    half = QK_ROPE // 2
    n = x.shape[1]
    from_upper = pltpu.roll(x, n - half, axis=1)
    from_lower = pltpu.roll(x, half, axis=1)
    return jnp.where(lane_in_head < QK_NOPE + half, from_upper, from_lower)


def _attn_in_kernel(x_ref, mod_ref, n1g_ref, w1_ref, qng_ref, wq_ref, kvng_ref, wkv_ref,
                    ck_ref, sk_ref,
                    q_ref, k_ref, v_ref, pu_ref, z_ref, cb_ref):
    x = x_ref[...]
    sh1, sc1 = mod_ref[0:1, :], mod_ref[1:2, :]
    h = _rms(x, n1g_ref[...]) * (1.0 + sc1) + sh1
    p = jnp.dot(h.astype(BF16), w1_ref[...], preferred_element_type=F32)
    o_ckv = Q_RANK
    o_kr = o_ckv + KV_RANK
    o_pu = o_kr + HEAD_PAD
    width = pu_ref.shape[-1]
    o_cx, o_cb, o_cc = o_pu + width, o_pu + 2 * width, o_pu + 3 * width
    cqn = _rms(p[:, 0:o_ckv], qng_ref[...]).astype(BF16)
    ckvn = _rms(p[:, o_ckv:o_kr], kvng_ref[...]).astype(BF16)
    q = jnp.dot(cqn, wq_ref[...], preferred_element_type=F32)
    kvp = jnp.dot(ckvn, wkv_ref[...], preferred_element_type=F32)
    ck, sk = ck_ref[...], sk_ref[...]
    lane = lax.broadcasted_iota(jnp.int32, ck.shape, 1)
    cq_tab = jnp.where(lane < QK_NOPE, 1.0, ck)
    scale = (QK_NOPE + QK_ROPE) ** -0.5 * float(np.log2(np.e))
    kr = p[:, o_kr:o_pu]
    krope = kr * ck + _rope_partner(kr, lane) * sk
    q_swap = _rope_partner(q, jnp.tile(lane, (1, N_HEADS)))
    for hd in range(N_HEADS):
        sl = slice(hd * HEAD_PAD, (hd + 1) * HEAD_PAD)
        q_ref[hd] = ((q[:, sl] * cq_tab + q_swap[:, sl] * sk) * scale).astype(BF16)
        k_ref[hd] = (kvp[:, sl] + krope).astype(BF16)
    v_ref[...] = kvp[:, N_HEADS * HEAD_PAD:].astype(BF16)
    pu_ref[...] = p[:, o_pu:o_cx]
    z_ref[...] = p[:, o_cc:o_cc + width] * p[:, o_cx:o_cb]
    cb_ref[...] = p[:, o_cb:o_cc]


def _attn_in(x, mod_l, n1g, w1, qng, wq, kvng, wkv, ck, sk, tm):
    B, S, D = x.shape
    width = (w1.shape[1] - Q_RANK - KV_RANK - HEAD_PAD) // 4
    tok = lambda b, i: (b, i, 0)
    full = lambda b, i: (0, 0)
    hspec = pl.BlockSpec((None, N_HEADS, tm, HEAD_PAD), lambda b, i: (b, 0, i, 0))
    wspec = lambda a: pl.BlockSpec(a.shape, full)
    f32o = jax.ShapeDtypeStruct((B, S, width), F32)
    return pl.pallas_call(
        _attn_in_kernel,
        grid=(B, S // tm),
        in_specs=[pl.BlockSpec((None, tm, D), tok),
                  pl.BlockSpec((None, 6, D), lambda b, i: (b, 0, 0)),
                  wspec(n1g), wspec(w1), wspec(qng), wspec(wq), wspec(kvng), wspec(wkv),
                  pl.BlockSpec((None, tm, HEAD_PAD), tok),
                  pl.BlockSpec((None, tm, HEAD_PAD), tok)],
        out_specs=[hspec, hspec,
                   pl.BlockSpec((None, tm, N_HEADS * V_HEAD), tok),
                   pl.BlockSpec((None, tm, width), tok),
                   pl.BlockSpec((None, tm, width), tok),
                   pl.BlockSpec((None, tm, width), tok)],
        out_shape=[jax.ShapeDtypeStruct((B, N_HEADS, S, HEAD_PAD), BF16),
                   jax.ShapeDtypeStruct((B, N_HEADS, S, HEAD_PAD), BF16),
                   jax.ShapeDtypeStruct((B, S, N_HEADS * V_HEAD), BF16),
                   f32o, f32o, f32o],
        compiler_params=_cparams("parallel", "parallel"),
        name="attn_in",
    )(x, mod_l, n1g, w1, qng, wq, kvng, wkv, ck, sk)


def _attention_kernel(q_ref, k_ref, v_ref, o_ref):
    nh = q_ref.shape[0]
    pair = 2 * V_HEAD
    scores = [lax.dot_general(q_ref[hd], k_ref[hd], (((1,), (1,)), ((), ())),
                              preferred_element_type=F32) for hd in range(nh)]
    outs = []
    for hd in range(nh):
        s = scores[hd]
        m = jnp.max(s, axis=-1, keepdims=True)
        p = jnp.exp2(s - m)
        l = jnp.sum(p, axis=-1, keepdims=True)
        v = v_ref[:, (hd // 2) * pair:(hd // 2 + 1) * pair]
        o = jnp.dot(p.astype(BF16), v, preferred_element_type=F32)
        outs.append(o / l)
    lane = lax.broadcasted_iota(jnp.int32, outs[0].shape, 1)
    for pr in range(nh // 2):
        o_ref[:, pr * pair:(pr + 1) * pair] = jnp.where(lane < V_HEAD, outs[2 * pr], outs[2 * pr + 1]).astype(BF16)


def _attention(q, k, v, tq, nh=4):
    B, H, S, _ = q.shape
    return pl.pallas_call(
        _attention_kernel,
        grid=(B, H // nh, S // tq),
        in_specs=[pl.BlockSpec((None, nh, tq, HEAD_PAD), lambda b, h, i: (b, h, i, 0)),
                  pl.BlockSpec((None, nh, S, HEAD_PAD), lambda b, h, i: (b, h, 0, 0)),
                  pl.BlockSpec((None, S, nh * V_HEAD), lambda b, h, i: (b, 0, h))],
        out_specs=pl.BlockSpec((None, tq, nh * V_HEAD), lambda b, h, i: (b, i, h)),
        out_shape=jax.ShapeDtypeStruct((B, S, H * V_HEAD), BF16),
        compiler_params=_cparams("parallel", "parallel", "parallel"),
        name="attention",
    )(q, k, v)


def _shift_rows(ext, shift, ts):
    n = ext.shape[0]
    return pltpu.roll(ext, shift % n, axis=0)[HALO:HALO + ts]


def _mix_post_kernel(x_ref, o_ref, pu_ref, pup_ref, pun_ref, z_ref, zp_ref, zn_ref, cb_ref,
                     mod_ref, n1g_ref, wg_ref, bg_ref, woa_ref, wpool_ref, pscale_ref, convw_ref, woc_ref,
                     wout_ref, n2g_ref, wrt_ref,
                     xo_ref, h2_ref, lg_ref, *, seq_len, sub):
    i = pl.program_id(1)
    last = pl.num_programs(1) - 1
    ts, D = x_ref.shape
    sh1, sc1, g1 = mod_ref[0:1, :], mod_ref[1:2, :], mod_ref[2:3, :]
    sh2, sc2 = mod_ref[3:4, :], mod_ref[4:5, :]
    first_f = jnp.where(i == 0, 0.0, 1.0)
    last_f = jnp.where(i == last, 0.0, 1.0)
    pext = jnp.concatenate([pup_ref[...] * first_f, pu_ref[...], pun_ref[...] * last_f], axis=0)
    zext = jnp.concatenate([zp_ref[...] * first_f, z_ref[...], zn_ref[...] * last_f], axis=0)
    wr = wrt_ref[...]
    wr_hi = wr.astype(BF16)
    wr_hl = jnp.concatenate([wr_hi, (wr - wr_hi.astype(F32)).astype(BF16)], axis=0)
    n_e = wr.shape[0]
    nt_dims = (((1,), (1,)), ((), ()))

    for r0 in range(0, ts, sub):
        rows = slice(r0, r0 + sub)
        x = x_ref[rows, :]
        h = (_rms(x, n1g_ref[...]) * (1.0 + sc1) + sh1).astype(BF16)
        gl = jnp.dot(h, wg_ref[...], preferred_element_type=F32) + bg_ref[...]
        gates = jax.nn.sigmoid(gl)
        ya = jnp.dot(o_ref[rows, :], woa_ref[...], preferred_element_type=F32)

        t = i * ts + r0 + lax.broadcasted_iota(jnp.int32, (sub, 1), 0)
        pe = pext[r0:r0 + sub + 2 * HALO]
        yb_parts = []
        for gi, w in enumerate(POOL_WINDOWS):
            gs = slice(gi * POOL_GROUP, (gi + 1) * POOL_GROUP)
            a = pe[:, gs]
            span = 1
            while span < w:
                a = a + pltpu.roll(a, span, axis=0)
                span *= 2
            wsum = _shift_rows(a, -(w // 2 - 1), sub)
            lo = jnp.maximum(t - w // 2, 0)
            hi = jnp.minimum(t + w // 2 - 1, seq_len - 1)
            cnt = (hi - lo + 1).astype(F32)
            mixed = wsum / cnt - pe[HALO:HALO + sub, gs]
            yb_parts.append(jnp.dot(mixed.astype(BF16), wpool_ref[gi], preferred_element_type=F32))
        yb = jnp.concatenate(yb_parts, axis=-1) * pscale_ref[...]

        ze = zext[r0:r0 + sub + 2 * HALO]
        yconv = (convw_ref[0:1, :] * _shift_rows(ze, 1, sub) + convw_ref[1:2, :] * ze[HALO:HALO + sub]
                 + convw_ref[2:3, :] * _shift_rows(ze, -1, sub))
        yc = jnp.dot((cb_ref[rows, :] * yconv).astype(BF16), woc_ref[...], preferred_element_type=F32)

        merged = gates[:, 0:D] * ya + gates[:, D:2 * D] * yb + gates[:, 2 * D:3 * D] * yc
        xn = x + g1 * jnp.dot(merged.astype(BF16), wout_ref[...], preferred_element_type=F32)
        xo_ref[rows, :] = xn
        h2 = _rms(xn, n2g_ref[...]) * (1.0 + sc2) + sh2
        h2_hi = h2.astype(BF16)
        h2_ref[rows, :] = h2_hi
        h2_lo = (h2 - h2_hi.astype(F32)).astype(BF16)
        r_hi = lax.dot_general(wr_hl, h2_hi, nt_dims, preferred_element_type=F32)
        r_lo = lax.dot_general(wr_hi, h2_lo, nt_dims, preferred_element_type=F32)
        lg_ref[:, rows] = r_hi[:n_e] + r_hi[n_e:] + r_lo


def _mix_post(x, o, pu, z, cb, mod_l, n1g, wg, bg, woa, wpool, pscale, convw, woc, wout, n2g, wrt, ts):
    B, S, D = x.shape
    W = pu.shape[-1]
    E = wrt.shape[0]
    nh = ts // HALO
    tok = lambda b, i: (b, i, 0)
    prev = lambda b, i: (b, jnp.maximum(i * nh - 1, 0), 0)
    nxt = lambda b, i: (b, jnp.minimum((i + 1) * nh, S // HALO - 1), 0)
    wspec = lambda a: pl.BlockSpec(a.shape, lambda b, i: (0,) * a.ndim)
    return pl.pallas_call(
        functools.partial(_mix_post_kernel, seq_len=S, sub=min(ts, 256)),
        grid=(B, S // ts),
        in_specs=[pl.BlockSpec((None, ts, D), tok),
                  pl.BlockSpec((None, ts, o.shape[-1]), tok),
                  pl.BlockSpec((None, ts, W), tok),
                  pl.BlockSpec((None, HALO, W), prev),
                  pl.BlockSpec((None, HALO, W), nxt),
                  pl.BlockSpec((None, ts, W), tok),
                  pl.BlockSpec((None, HALO, W), prev),
                  pl.BlockSpec((None, HALO, W), nxt),
                  pl.BlockSpec((None, ts, W), tok),
                  pl.BlockSpec((None, 6, D), lambda b, i: (b, 0, 0)),
                  wspec(n1g), wspec(wg), wspec(bg), wspec(woa), wspec(wpool), wspec(pscale), wspec(convw),
                  wspec(woc), wspec(wout), wspec(n2g), wspec(wrt)],
        out_specs=[pl.BlockSpec((None, ts, D), tok),
                   pl.BlockSpec((None, ts, D), tok),
                   pl.BlockSpec((None, E, ts), lambda b, i: (b, 0, i))],
        out_shape=[jax.ShapeDtypeStruct((B, S, D), F32),
                   jax.ShapeDtypeStruct((B, S, D), BF16),
                   jax.ShapeDtypeStruct((B, E, S), F32)],
        compiler_params=_cparams("parallel", "parallel"),
        name="mix_post",
    )(x, o, pu, pu, pu, z, z, z, cb, mod_l, n1g, wg, bg, woa, wpool, pscale, convw, woc, wout, n2g, wrt)


def _prefix_chunks(flags, tri):
    E, S = flags.shape
    running = jnp.zeros((E, 1), F32)
    pieces, starts = [], []
    for c in range(S // TOKEN_BLOCK):
        blk = flags[:, c * TOKEN_BLOCK:(c + 1) * TOKEN_BLOCK]
        incl = jnp.dot(blk.astype(BF16), tri, preferred_element_type=F32)
        pieces.append(incl - blk + running)
        starts.append(running)
        running = running + jnp.sum(blk, axis=-1, keepdims=True)
    starts.append(running)
    return jnp.concatenate(pieces, axis=-1), starts


def _route_kernel(lg_ref, slot_ref, wts_ref, cnt_ref, *, cap):
    lg = lg_ref[...]
    E, S = lg.shape
    m = jnp.max(lg, axis=0, keepdims=True)
    ex = jnp.exp(lg - m)
    aff = ex / jnp.sum(ex, axis=0, keepdims=True)

    def step(it, tau_bits):
        cand = tau_bits | jnp.left_shift(jnp.int32(1), 30 - it)
        n = jnp.sum((aff >= lax.bitcast_convert_type(cand, F32)).astype(jnp.int32), axis=-1, keepdims=True)
        return jnp.where(n >= cap, cand, tau_bits)

    tau = lax.bitcast_convert_type(lax.fori_loop(0, 31, step, jnp.zeros((E, 1), jnp.int32)), F32)
    gt = aff > tau
    eq = aff == tau
    need = cap - jnp.sum(gt.astype(jnp.int32), axis=-1, keepdims=True)
    row = lax.broadcasted_iota(jnp.int32, (TOKEN_BLOCK, TOKEN_BLOCK), 0)
    col = lax.broadcasted_iota(jnp.int32, (TOKEN_BLOCK, TOKEN_BLOCK), 1)
    tri = jnp.where(row <= col, 1.0, 0.0).astype(BF16)
    eq_rank, _ = _prefix_chunks(jnp.where(eq, 1.0, 0.0), tri)
    sel = gt | (eq & (eq_rank.astype(jnp.int32) < need))
    pos, starts = _prefix_chunks(jnp.where(sel, 1.0, 0.0), tri)
    slot_ref[...] = jnp.where(sel, pos.astype(jnp.int32), -1)
    wts_ref[...] = jnp.where(sel, aff, 0.0)
    lane = lax.broadcasted_iota(jnp.int32, (E, LANES), 1)
    cnt = jnp.full((E, LANES), cap, jnp.int32)
    for c, st in enumerate(starts):
        cnt = jnp.where(lane == c, st.astype(jnp.int32), cnt)
    cnt_ref[...] = cnt


def _route(logits_t, cap):
    B, E, S = logits_t.shape
    spec = pl.BlockSpec((None, E, S), lambda b: (b, 0, 0))
    return pl.pallas_call(
        functools.partial(_route_kernel, cap=cap),
        grid=(B,),
        in_specs=[spec],
        out_specs=[spec, spec, pl.BlockSpec((None, E, LANES), lambda b: (b, 0, 0))],
        out_shape=[jax.ShapeDtypeStruct((B, E, S), jnp.int32),
                   jax.ShapeDtypeStruct((B, E, S), F32),
                   jax.ShapeDtypeStruct((B, E, LANES), jnp.int32)],
        compiler_params=_cparams("parallel"),
        name="route",
    )(logits_t)


def _window_start(c0, cap):
    a0 = jnp.left_shift(jnp.right_shift(c0, SLOT_ALIGN_LOG2), SLOT_ALIGN_LOG2)
    return pl.multiple_of(jnp.minimum(a0, cap - SLOT_WINDOW), 1 << SLOT_ALIGN_LOG2)


def _gather_kernel(cnt_ref, slot_ref, wts_ref, h2_ref, xg_ref, tv_ref):
    b, c = pl.program_id(0), pl.program_id(1)
    E, cap, D = xg_ref.shape
    win = SLOT_WINDOW

    @pl.when(c == 0)
    def _():
        xg_ref[...] = jnp.zeros(xg_ref.shape, BF16)
        tv_ref[...] = jnp.zeros(tv_ref.shape, F32)

    htile = h2_ref[...]
    p_iota = lax.broadcasted_iota(jnp.int32, (win, TOKEN_BLOCK), 0)
    first = [cnt_ref[(b * E + e) * CNT_PAD + c] for e in range(E)]
    starts = [_window_start(first[e], cap) for e in range(E)]

    def add_rows(e, a, hit, rows):
        sl = pl.ds(a, win)
        xg_ref[e, sl, :] = (xg_ref[e, sl, :].astype(F32) + rows).astype(BF16)
        tv_ref[e, sl, :] += jnp.sum(jnp.where(hit, wts_ref[e:e + 1, :], 0.0), axis=1, keepdims=True)

    hits = [(p_iota + starts[e]) == slot_ref[e:e + 1, :] for e in range(E)]
    onehot = jnp.concatenate([jnp.where(h, 1.0, 0.0).astype(BF16) for h in hits], axis=0)
    rows = jnp.dot(onehot, htile, preferred_element_type=F32)
    for e in range(E):
        add_rows(e, starts[e], hits[e], rows[e * win:(e + 1) * win])

    for e in range(E):
        last = cnt_ref[(b * E + e) * CNT_PAD + c + 1]

        @pl.when(last > starts[e] + win)
        def _():
            def more(k, carry):
                lo = starts[e] + k * win
                a = pl.multiple_of(jnp.minimum(lo, cap - win), 1 << SLOT_ALIGN_LOG2)
                hit = ((p_iota + a) == slot_ref[e:e + 1, :]) & ((p_iota + a) >= lo)
                add_rows(e, a, hit, jnp.dot(jnp.where(hit, 1.0, 0.0).astype(BF16), htile,
                                            preferred_element_type=F32))
                return carry

            lax.fori_loop(1, (last - starts[e] + win - 1) // win, more, 0)


def _gather(cnt_flat, slot, wts, h2, cap):
    B, E, S = slot.shape
    D = h2.shape[-1]
    assert cap % SLOT_WINDOW == 0
    tile = lambda b, c, cnt: (b, 0, c)
    return pl.pallas_call(
        _gather_kernel,
        grid_spec=pltpu.PrefetchScalarGridSpec(
            num_scalar_prefetch=1,
            grid=(B, S // TOKEN_BLOCK),
            in_specs=[pl.BlockSpec((None, E, TOKEN_BLOCK), tile),
                      pl.BlockSpec((None, E, TOKEN_BLOCK), tile),
                      pl.BlockSpec((None, TOKEN_BLOCK, D), lambda b, c, cnt: (b, c, 0))],
            out_specs=[pl.BlockSpec((None, E, cap, D), lambda b, c, cnt: (b, 0, 0, 0)),
                       pl.BlockSpec((None, E, cap, 1), lambda b, c, cnt: (b, 0, 0, 0))]),
        out_shape=[jax.ShapeDtypeStruct((B, E, cap, D), BF16),
                   jax.ShapeDtypeStruct((B, E, cap, 1), F32)],
        compiler_params=_cparams("parallel", "arbitrary"),
        name="gather",
    )(cnt_flat, slot, wts, h2)


def _experts_kernel(xg_ref, tv_ref, wg_ref, wu_ref, wd_ref, y_ref):
    wg, wu, wd = wg_ref[...].astype(BF16), wu_ref[...].astype(BF16), wd_ref[...].astype(BF16)
    for b in range(xg_ref.shape[0]):
        xg = xg_ref[b]
        g = jnp.dot(xg, wg, preferred_element_type=F32)
        u = jnp.dot(xg, wu, preferred_element_type=F32)
        a = (g * jax.nn.sigmoid(g) * u).astype(BF16)
        y_ref[b] = (jnp.dot(a, wd, preferred_element_type=F32) * tv_ref[b]).astype(BF16)


def _experts(xg, tv, w_gate, w_up, w_down, layer):
    B, E, cap, D = xg.shape
    F = w_gate.shape[-1]
    tok = pl.BlockSpec((B, None, cap, D), lambda e: (0, e, 0, 0))
    return pl.pallas_call(
        _experts_kernel,
        grid=(E,),
        in_specs=[tok,
                  pl.BlockSpec((B, None, cap, 1), lambda e: (0, e, 0, 0)),
                  pl.BlockSpec((None, None, D, F), lambda e: (layer, e, 0, 0)),
                  pl.BlockSpec((None, None, D, F), lambda e: (layer, e, 0, 0)),
                  pl.BlockSpec((None, None, F, D), lambda e: (layer, e, 0, 0))],
        out_specs=tok,
        out_shape=jax.ShapeDtypeStruct((B, E, cap, D), BF16),
        compiler_params=_cparams("parallel"),
        name="experts",
    )(xg, tv, w_gate, w_up, w_down)


def _combine_kernel(cnt_ref, slot_ref, y_ref, x_ref, mod_ref, fg_ref, o_ref, acc_ref, *, final_norm):
    b, c = pl.program_id(0), pl.program_id(1)
    E, cap, D = y_ref.shape
    win = SLOT_WINDOW
    first = [cnt_ref[(b * E + e) * CNT_PAD + c] for e in range(E)]
    starts = [_window_start(first[e], cap) for e in range(E)]
    ycat = jnp.concatenate([y_ref[e, pl.ds(starts[e], win), :] for e in range(E)], axis=0)
    p_iota = lax.broadcasted_iota(jnp.int32, (win, TOKEN_BLOCK), 0)
    contract_slots = (((0,), (0,)), ((), ()))
    onehot = jnp.concatenate([jnp.where((p_iota + starts[e]) == slot_ref[e:e + 1, :], 1.0, 0.0).astype(BF16)
                              for e in range(E)], axis=0)
    acc_ref[...] = lax.dot_general(onehot, ycat, contract_slots, preferred_element_type=F32)

    for e in range(E):
        last = cnt_ref[(b * E + e) * CNT_PAD + c + 1]

        @pl.when(last > starts[e] + win)
        def _():
            def more(k, carry):
                lo = starts[e] + k * win
                a = pl.multiple_of(jnp.minimum(lo, cap - win), 1 << SLOT_ALIGN_LOG2)
                hit = ((p_iota + a) == slot_ref[e:e + 1, :]) & ((p_iota + a) >= lo)
                acc_ref[...] += lax.dot_general(jnp.where(hit, 1.0, 0.0).astype(BF16), y_ref[e, pl.ds(a, win), :],
                                                contract_slots, preferred_element_type=F32)
                return carry

            lax.fori_loop(1, (last - starts[e] + win - 1) // win, more, 0)

    out = x_ref[...] + mod_ref[5:6, :] * acc_ref[...]
    if final_norm:
        out = _rms(out, fg_ref[...])
    o_ref[...] = out


def _combine(cnt_flat, slot, y, x, mod_l, final_g, final_norm):
    B, S, D = x.shape
    E, cap = y.shape[1], y.shape[2]
    tok = lambda b, c, cnt: (b, c, 0)
    return pl.pallas_call(
        functools.partial(_combine_kernel, final_norm=final_norm),
        grid_spec=pltpu.PrefetchScalarGridSpec(
            num_scalar_prefetch=1,
            grid=(B, S // TOKEN_BLOCK),
            in_specs=[pl.BlockSpec((None, E, TOKEN_BLOCK), lambda b, c, cnt: (b, 0, c)),
                      pl.BlockSpec((None, E, cap, D), lambda b, c, cnt: (b, 0, 0, 0)),
                      pl.BlockSpec((None, TOKEN_BLOCK, D), tok),
                      pl.BlockSpec((None, 6, D), lambda b, c, cnt: (b, 0, 0)),
                      pl.BlockSpec((1, D), lambda b, c, cnt: (0, 0))],
            out_specs=pl.BlockSpec((None, TOKEN_BLOCK, D), tok),
            scratch_shapes=[pltpu.VMEM((TOKEN_BLOCK, D), F32)]),
        out_shape=jax.ShapeDtypeStruct((B, S, D), F32),
        compiler_params=_cparams("parallel", "arbitrary"),
        name="combine",
    )(cnt_flat, slot, y, x, mod_l, final_g)


def _prep_layer(l, w_in, w_uq, w_ukv, w_oa, w_pool, w_oc, w_out, w_router, width):
    D = w_in.shape[1]
    half = QK_ROPE // 2
    o_ckv = Q_RANK
    o_kr = o_ckv + KV_RANK
    o_pu = o_kr + QK_ROPE
    o_gl = o_pu + 4 * width
    wi = w_in[l]
    kr = wi[:, o_kr:o_pu]
    z64 = jnp.zeros((D, QK_NOPE), F32)
    z32 = jnp.zeros((D, HEAD_PAD - QK_NOPE - QK_ROPE), F32)
    kr_pad = jnp.concatenate([z64, kr, z32], axis=1)
    w1 = jnp.concatenate([wi[:, :o_kr], kr_pad, wi[:, o_pu:o_gl]], axis=1).astype(BF16)
    wg = wi[:, o_gl:].astype(BF16)

    dqk = QK_NOPE + QK_ROPE
    q3 = w_uq[l].reshape(Q_RANK, N_HEADS, dqk)
    zq = jnp.zeros((Q_RANK, N_HEADS, HEAD_PAD - dqk), F32)
    wq = jnp.concatenate([q3, zq], axis=-1).reshape(Q_RANK, N_HEADS * HEAD_PAD).astype(BF16)

    kv3 = w_ukv[l].reshape(KV_RANK, N_HEADS, QK_NOPE + V_HEAD)
    wk = jnp.concatenate([kv3[..., :QK_NOPE], jnp.zeros((KV_RANK, N_HEADS, HEAD_PAD - QK_NOPE), F32)], axis=-1)
    wkv = jnp.concatenate([wk.reshape(KV_RANK, N_HEADS * HEAD_PAD),
                           kv3[..., QK_NOPE:].reshape(KV_RANK, N_HEADS * V_HEAD)], axis=1).astype(BF16)
    return dict(w1=w1, wg=wg, wq=wq, wkv=wkv,
                woa=w_oa[l].astype(BF16), wpool=w_pool[l].astype(BF16), woc=w_oc[l].astype(BF16),
                wout=w_out[l].astype(BF16), wrt=w_router[l].T)


def kernel(x, c, positions, w_mod, b_mod, norm1_g, w_in, b_gate, q_norm_g, w_uq, kv_norm_g, w_ukv, w_oa, w_pool,
           pool_scale, conv_w, w_oc, w_out, norm2_g, w_router, w_gate, w_up, w_down, final_g):
    B, S, D = x.shape
    L = w_mod.shape[0]
    E = w_router.shape[-1]
    width = w_oc.shape[1]
    cap = EC_FACTOR * S // E
    assert S % TOKEN_BLOCK == 0 and S // TOKEN_BLOCK + 1 <= CNT_PAD
    tm = min(S, 512)
    tq = min(S, 256)

    mod = _modulation(c, w_mod, b_mod).reshape(L, B, 6, D)
    ck, sk = _rope_tables(positions)
    row = lambda a: a.reshape(1, -1)
    for l in range(L):
        w = _prep_layer(l, w_in, w_uq, w_ukv, w_oa, w_pool, w_oc, w_out, w_router, width)
        q, k, v, pu, z, cb = _attn_in(x, mod[l], row(norm1_g[l]), w["w1"], row(q_norm_g[l]), w["wq"],
                                      row(kv_norm_g[l]), w["wkv"], ck, sk, tm)
        o = _attention(q, k, v, tq)
        x, h2, logits_t = _mix_post(x, o, pu, z, cb, mod[l], row(norm1_g[l]), w["wg"], row(b_gate[l]), w["woa"],
                                    w["wpool"], row(pool_scale[l]), conv_w[l], w["woc"], w["wout"],
                                    row(norm2_g[l]), w["wrt"], tm)
        slot, wts, cnt = _route(logits_t, cap)
        cnt_flat = cnt[:, :, :CNT_PAD].reshape(-1)
        xg, tv = _gather(cnt_flat, slot, wts, h2, cap)
        y = _experts(xg, tv, w_gate, w_up, w_down, l)
        x = _combine(cnt_flat, slot, y, x, mod[l], row(final_g), final_norm=(l == L - 1))
    return x
```

```python
import functools

import jax
import jax.numpy as jnp
import numpy as np
from jax import lax
from jax.experimental import pallas as pl
from jax.experimental.pallas import tpu as pltpu

N_HEADS = 8
QK_NOPE = 64
QK_ROPE = 32
V_HEAD = 64
Q_RANK = 384
KV_RANK = 256
ROPE_THETA = 10000.0
POOL_WINDOWS = (2, 4, 8, 16)
POOL_GROUP = 128
CONV_K = 3
N_BRANCH = 3
N_EXPERTS = 16
EC_FACTOR = 2
EPS = 1e-6

LANES = 128
SUBLANES = 8
HEAD_PAD = 128
HALO = 16
SLOT_WINDOW = 64
SLOT_ALIGN_LOG2 = 4
TOKEN_BLOCK = 256
MOE_TILES_PER_STEP = 2
CNT_PAD = 32
VMEM_LIMIT = 56 * 1024 * 1024

BF16 = jnp.bfloat16
F32 = jnp.float32


def _cparams(*sem):
    return pltpu.CompilerParams(dimension_semantics=sem, vmem_limit_bytes=VMEM_LIMIT)


def _layer_spec(a, layer):
    zeros = (0,) * (a.ndim - 1)
    return pl.BlockSpec((None,) + a.shape[1:], lambda *_: (layer,) + zeros)


def _mod_spec(mod, layer):
    return pl.BlockSpec((None, None) + mod.shape[2:], lambda b, *_: (layer, b, 0, 0))


def _rms(x, g):
    return x * lax.rsqrt(jnp.mean(x * x, axis=-1, keepdims=True) + EPS) * g


def _split_bf16(a):
    hi = a.astype(BF16)
    return hi, (a - hi.astype(F32)).astype(BF16)


def _mod_kernel(c_ref, w_ref, b_ref, o_ref):
    c = c_ref[...]
    c_hi, c_lo = _split_bf16(c * jax.nn.sigmoid(c))
    w_hi, w_lo = _split_bf16(w_ref[...])
    rows = c.shape[0]
    r = jnp.dot(jnp.concatenate([c_hi, c_lo], axis=0), w_hi, preferred_element_type=F32)
    out = r[:rows] + r[rows:] + jnp.dot(c_hi, w_lo, preferred_element_type=F32)
    o_ref[...] = out[:o_ref.shape[0]] + b_ref[...]


def _modulation(c, w_mod, b_mod):
    L, D, N = w_mod.shape
    B = c.shape[0]
    tn = 1536
    rows = -(-B // SUBLANES) * SUBLANES
    c = jnp.pad(c, ((0, rows - B), (0, 0)))
    return pl.pallas_call(
        _mod_kernel,
        grid=(L, N // tn),
        in_specs=[pl.BlockSpec((rows, D), lambda l, j: (0, 0)),
                  pl.BlockSpec((None, D, tn), lambda l, j: (l, 0, j)),
                  pl.BlockSpec((None, 1, tn), lambda l, j: (l, 0, j))],
        out_specs=pl.BlockSpec((None, B, tn), lambda l, j: (l, 0, j)),
        out_shape=jax.ShapeDtypeStruct((L, B, N), F32),
        compiler_params=_cparams("parallel", "parallel"),
        name="modulation",
    )(c, w_mod, b_mod.reshape(L, 1, N))


def _rope_kernel(pos_ref, freq_ref, ck_ref, sk_ref):
    ang = pos_ref[...].astype(F32) * freq_ref[...]
    lane = lax.broadcasted_iota(jnp.int32, ang.shape, 1)
    cosv, sinv = jnp.cos(ang), jnp.sin(ang)
    half = QK_ROPE // 2
    in_rope = (lane >= QK_NOPE) & (lane < QK_NOPE + QK_ROPE)
    ck_ref[...] = jnp.where(in_rope, cosv, 0.0)
    sk_ref[...] = jnp.where(in_rope, jnp.where(lane < QK_NOPE + half, -sinv, sinv), 0.0)


def _rope_tables(positions):
    B, S = positions.shape
    half = QK_ROPE // 2
    freqs = ROPE_THETA ** (-jnp.arange(0, QK_ROPE, 2, dtype=F32) / QK_ROPE)
    freq_row = jnp.concatenate([jnp.zeros((QK_NOPE,), F32), freqs, freqs,
                                jnp.zeros((HEAD_PAD - QK_NOPE - 2 * half,), F32)]).reshape(1, HEAD_PAD)
    ts = min(S, 1024)
    out = jax.ShapeDtypeStruct((B, S, HEAD_PAD), F32)
    return pl.pallas_call(
        _rope_kernel,
        grid=(B, S // ts),
        in_specs=[pl.BlockSpec((None, ts, 1), lambda b, i: (b, i, 0)),
                  pl.BlockSpec((1, HEAD_PAD), lambda b, i: (0, 0))],
        out_specs=[pl.BlockSpec((None, ts, HEAD_PAD), lambda b, i: (b, i, 0))] * 2,
        out_shape=[out, out],
        compiler_params=_cparams("parallel", "parallel"),
        name="rope_tables",
    )(positions.reshape(B, S, 1), freq_row)


def _rope_partner(x, lane_in_head):
    half = QK_ROPE // 2
    n = x.shape[1]
    from_upper = pltpu.roll(x, n - half, axis=1)
    from_lower = pltpu.roll(x, half, axis=1)
    return jnp.where(lane_in_head < QK_NOPE + half, from_upper, from_lower)


def _attn_in_kernel(x_ref, mod_ref, n1g_ref, w1_ref, qng_ref, wq_ref, kvng_ref, wkv_ref,
                    ck_ref, sk_ref,
                    q_ref, k_ref, v_ref, pu_ref, z_ref, cb_ref):
    x = x_ref[...]
    sh1, sc1 = mod_ref[0:1, :], mod_ref[1:2, :]
    h = _rms(x, n1g_ref[...]) * (1.0 + sc1) + sh1
    p = jnp.dot(h.astype(BF16), w1_ref[...], preferred_element_type=F32)
    o_ckv = Q_RANK
    o_kr = o_ckv + KV_RANK
    o_pu = o_kr + HEAD_PAD
    width = pu_ref.shape[-1]
    o_cx, o_cb, o_cc = o_pu + width, o_pu + 2 * width, o_pu + 3 * width
    cqn = _rms(p[:, 0:o_ckv], qng_ref[...]).astype(BF16)
    ckvn = _rms(p[:, o_ckv:o_kr], kvng_ref[...]).astype(BF16)
    q = jnp.dot(cqn, wq_ref[...], preferred_element_type=F32)
    kvp = jnp.dot(ckvn, wkv_ref[...], preferred_element_type=F32)
    ck, sk = ck_ref[...], sk_ref[...]
    lane = lax.broadcasted_iota(jnp.int32, ck.shape, 1)
    cq_tab = jnp.where(lane < QK_NOPE, 1.0, ck)
    scale = (QK_NOPE + QK_ROPE) ** -0.5 * float(np.log2(np.e))
    kr = p[:, o_kr:o_pu]
    krope = kr * ck + _rope_partner(kr, lane) * sk
    q_swap = _rope_partner(q, jnp.tile(lane, (1, N_HEADS)))
    for hd in range(N_HEADS):
        sl = slice(hd * HEAD_PAD, (hd + 1) * HEAD_PAD)
        q_ref[hd] = ((q[:, sl] * cq_tab + q_swap[:, sl] * sk) * scale).astype(BF16)
        k_ref[hd] = (kvp[:, sl] + krope).astype(BF16)
    v_ref[...] = kvp[:, N_HEADS * HEAD_PAD:].astype(BF16)
    pu_ref[...] = p[:, o_pu:o_cx]
    z_ref[...] = p[:, o_cc:o_cc + width] * p[:, o_cx:o_cb]
    cb_ref[...] = p[:, o_cb:o_cc]


def _attn_in(x, mod, layer, n1g, w1, qng, wq, kvng, wkv, ck, sk, tm):
    B, S, D = x.shape
    width = (w1.shape[-1] - Q_RANK - KV_RANK - HEAD_PAD) // 4
    tok = lambda b, i: (b, i, 0)
    hspec = pl.BlockSpec((None, N_HEADS, tm, HEAD_PAD), lambda b, i: (b, 0, i, 0))
    wspec = lambda a: _layer_spec(a, layer)
    f32o = jax.ShapeDtypeStruct((B, S, width), F32)
    return pl.pallas_call(
        _attn_in_kernel,
        grid=(B, S // tm),
        in_specs=[pl.BlockSpec((None, tm, D), tok),
                  _mod_spec(mod, layer),
                  wspec(n1g), wspec(w1), wspec(qng), wspec(wq), wspec(kvng), wspec(wkv),
                  pl.BlockSpec((None, tm, HEAD_PAD), tok),
                  pl.BlockSpec((None, tm, HEAD_PAD), tok)],
        out_specs=[hspec, hspec,
                   pl.BlockSpec((None, tm, N_HEADS * V_HEAD), tok),
                   pl.BlockSpec((None, tm, width), tok),
                   pl.BlockSpec((None, tm, width), tok),
                   pl.BlockSpec((None, tm, width), tok)],
        out_shape=[jax.ShapeDtypeStruct((B, N_HEADS, S, HEAD_PAD), BF16),
                   jax.ShapeDtypeStruct((B, N_HEADS, S, HEAD_PAD), BF16),
                   jax.ShapeDtypeStruct((B, S, N_HEADS * V_HEAD), BF16),
                   f32o, f32o, f32o],
        compiler_params=_cparams("parallel", "parallel"),
        name="attn_in",
    )(x, mod, n1g, w1, qng, wq, kvng, wkv, ck, sk)


def _attention_kernel(q_ref, k_ref, v_ref, o_ref):
    nh = q_ref.shape[0]
    pair = 2 * V_HEAD
    scores = [lax.dot_general(q_ref[hd], k_ref[hd], (((1,), (1,)), ((), ())),
                              preferred_element_type=F32) for hd in range(nh)]
    outs = []
    for hd in range(nh):
        s = scores[hd]
        m = jnp.max(s, axis=-1, keepdims=True)
        p = jnp.exp2(s - m)
        l = jnp.sum(p, axis=-1, keepdims=True)
        v = v_ref[:, (hd // 2) * pair:(hd // 2 + 1) * pair]
        o = jnp.dot(p.astype(BF16), v, preferred_element_type=F32)
        outs.append(o / l)
    lane = lax.broadcasted_iota(jnp.int32, outs[0].shape, 1)
    for pr in range(nh // 2):
        o_ref[:, pr * pair:(pr + 1) * pair] = jnp.where(lane < V_HEAD, outs[2 * pr], outs[2 * pr + 1]).astype(BF16)


def _attention(q, k, v, tq, nh=4):
    B, H, S, _ = q.shape
    return pl.pallas_call(
        _attention_kernel,
        grid=(B, H // nh, S // tq),
        in_specs=[pl.BlockSpec((None, nh, tq, HEAD_PAD), lambda b, h, i: (b, h, i, 0)),
                  pl.BlockSpec((None, nh, S, HEAD_PAD), lambda b, h, i: (b, h, 0, 0)),
                  pl.BlockSpec((None, S, nh * V_HEAD), lambda b, h, i: (b, 0, h))],
        out_specs=pl.BlockSpec((None, tq, nh * V_HEAD), lambda b, h, i: (b, i, h)),
        out_shape=jax.ShapeDtypeStruct((B, S, H * V_HEAD), BF16),
        compiler_params=_cparams("parallel", "parallel", "parallel"),
        name="attention",
    )(q, k, v)


def _shift_rows(ext, shift, ts):
    n = ext.shape[0]
    return pltpu.roll(ext, shift % n, axis=0)[HALO:HALO + ts]


def _mix_post_kernel(x_ref, o_ref, pu_ref, pup_ref, pun_ref, z_ref, zp_ref, zn_ref, cb_ref,
                     mod_ref, n1g_ref, wg_ref, bg_ref, woa_ref, wpool_ref, pscale_ref, convw_ref, woc_ref,
                     wout_ref, n2g_ref, wrt_ref,
                     xo_ref, h2_ref, lg_ref, *, seq_len, sub):
    i = pl.program_id(1)
    last = pl.num_programs(1) - 1
    ts, D = x_ref.shape
    sh1, sc1, g1 = mod_ref[0:1, :], mod_ref[1:2, :], mod_ref[2:3, :]
    sh2, sc2 = mod_ref[3:4, :], mod_ref[4:5, :]
    first_f = jnp.where(i == 0, 0.0, 1.0)
    last_f = jnp.where(i == last, 0.0, 1.0)
    pext = jnp.concatenate([pup_ref[...] * first_f, pu_ref[...], pun_ref[...] * last_f], axis=0)
    zext = jnp.concatenate([zp_ref[...] * first_f, z_ref[...], zn_ref[...] * last_f], axis=0)
    wr = wrt_ref[...]
    wr_hi = wr.astype(BF16)
    wr_hl = jnp.concatenate([wr_hi, (wr - wr_hi.astype(F32)).astype(BF16)], axis=0)
    n_e = wr.shape[0]
    nt_dims = (((1,), (1,)), ((), ()))

    for r0 in range(0, ts, sub):
        rows = slice(r0, r0 + sub)
        x = x_ref[rows, :]
        h = (_rms(x, n1g_ref[...]) * (1.0 + sc1) + sh1).astype(BF16)
        gl = jnp.dot(h, wg_ref[...], preferred_element_type=F32) + bg_ref[...]
        gates = jax.nn.sigmoid(gl)
        ya = jnp.dot(o_ref[rows, :], woa_ref[...], preferred_element_type=F32)

        t = i * ts + r0 + lax.broadcasted_iota(jnp.int32, (sub, 1), 0)
        pe = pext[r0:r0 + sub + 2 * HALO]
        yb_parts = []
        for gi, w in enumerate(POOL_WINDOWS):
            gs = slice(gi * POOL_GROUP, (gi + 1) * POOL_GROUP)
            a = pe[:, gs]
            span = 1
            while span < w:
                a = a + pltpu.roll(a, span, axis=0)
                span *= 2
            wsum = _shift_rows(a, -(w // 2 - 1), sub)
            lo = jnp.maximum(t - w // 2, 0)
            hi = jnp.minimum(t + w // 2 - 1, seq_len - 1)
            cnt = (hi - lo + 1).astype(F32)
            mixed = wsum / cnt - pe[HALO:HALO + sub, gs]
            yb_parts.append(jnp.dot(mixed.astype(BF16), wpool_ref[gi], preferred_element_type=F32))
        yb = jnp.concatenate(yb_parts, axis=-1) * pscale_ref[...]

        ze = zext[r0:r0 + sub + 2 * HALO]
        yconv = (convw_ref[0:1, :] * _shift_rows(ze, 1, sub) + convw_ref[1:2, :] * ze[HALO:HALO + sub]
                 + convw_ref[2:3, :] * _shift_rows(ze, -1, sub))
        yc = jnp.dot((cb_ref[rows, :] * yconv).astype(BF16), woc_ref[...], preferred_element_type=F32)

        merged = gates[:, 0:D] * ya + gates[:, D:2 * D] * yb + gates[:, 2 * D:3 * D] * yc
        xn = x + g1 * jnp.dot(merged.astype(BF16), wout_ref[...], preferred_element_type=F32)
        xo_ref[rows, :] = xn
        h2 = _rms(xn, n2g_ref[...]) * (1.0 + sc2) + sh2
        h2_hi = h2.astype(BF16)
        h2_ref[rows, :] = h2_hi
        h2_lo = (h2 - h2_hi.astype(F32)).astype(BF16)
        r_hi = lax.dot_general(wr_hl, h2_hi, nt_dims, preferred_element_type=F32)
        r_lo = lax.dot_general(wr_hi, h2_lo, nt_dims, preferred_element_type=F32)
        lg_ref[:, rows] = r_hi[:n_e] + r_hi[n_e:] + r_lo


def _mix_post(x, o, pu, z, cb, mod, layer, n1g, wg, bg, woa, wpool, pscale, convw, woc, wout, n2g, wrt, ts):
    B, S, D = x.shape
    W = pu.shape[-1]
    E = wrt.shape[1]
    nh = ts // HALO
    tok = lambda b, i: (b, i, 0)
    prev = lambda b, i: (b, jnp.maximum(i * nh - 1, 0), 0)
    nxt = lambda b, i: (b, jnp.minimum((i + 1) * nh, S // HALO - 1), 0)
    wspec = lambda a: _layer_spec(a, layer)
    return pl.pallas_call(
        functools.partial(_mix_post_kernel, seq_len=S, sub=min(ts, 256)),
        grid=(B, S // ts),
        in_specs=[pl.BlockSpec((None, ts, D), tok),
                  pl.BlockSpec((None, ts, o.shape[-1]), tok),
                  pl.BlockSpec((None, ts, W), tok),
                  pl.BlockSpec((None, HALO, W), prev),
                  pl.BlockSpec((None, HALO, W), nxt),
                  pl.BlockSpec((None, ts, W), tok),
                  pl.BlockSpec((None, HALO, W), prev),
                  pl.BlockSpec((None, HALO, W), nxt),
                  pl.BlockSpec((None, ts, W), tok),
                  _mod_spec(mod, layer),
                  wspec(n1g), wspec(wg), wspec(bg), wspec(woa), wspec(wpool), wspec(pscale), wspec(convw),
                  wspec(woc), wspec(wout), wspec(n2g), wspec(wrt)],
        out_specs=[pl.BlockSpec((None, ts, D), tok),
                   pl.BlockSpec((None, ts, D), tok),
                   pl.BlockSpec((None, E, ts), lambda b, i: (b, 0, i))],
        out_shape=[jax.ShapeDtypeStruct((B, S, D), F32),
                   jax.ShapeDtypeStruct((B, S, D), BF16),
                   jax.ShapeDtypeStruct((B, E, S), F32)],
        compiler_params=_cparams("parallel", "parallel"),
        name="mix_post",
    )(x, o, pu, pu, pu, z, z, z, cb, mod, n1g, wg, bg, woa, wpool, pscale, convw, woc, wout, n2g, wrt)


def _prefix_chunks(flags, tri):
    E, S = flags.shape
    running = jnp.zeros((E, 1), F32)
    pieces, starts = [], []
    for c in range(S // TOKEN_BLOCK):
        blk = flags[:, c * TOKEN_BLOCK:(c + 1) * TOKEN_BLOCK]
        incl = jnp.dot(blk.astype(BF16), tri, preferred_element_type=F32)
        pieces.append(incl - blk + running)
        starts.append(running)
        running = running + jnp.sum(blk, axis=-1, keepdims=True)
    starts.append(running)
    return jnp.concatenate(pieces, axis=-1), starts


def _route_kernel(lg_ref, slot_ref, wts_ref, cnt_ref, *, cap):
    lg = lg_ref[...]
    E, S = lg.shape
    m = jnp.max(lg, axis=0, keepdims=True)
    ex = jnp.exp(lg - m)
    aff = ex / jnp.sum(ex, axis=0, keepdims=True)

    def step(it, tau_bits):
        cand = tau_bits | jnp.left_shift(jnp.int32(1), 30 - it)
        n = jnp.sum((aff >= lax.bitcast_convert_type(cand, F32)).astype(jnp.int32), axis=-1, keepdims=True)
        return jnp.where(n >= cap, cand, tau_bits)

    tau = lax.bitcast_convert_type(lax.fori_loop(0, 31, step, jnp.zeros((E, 1), jnp.int32)), F32)
    gt = aff > tau
    eq = aff == tau
    need = cap - jnp.sum(gt.astype(jnp.int32), axis=-1, keepdims=True)
    row = lax.broadcasted_iota(jnp.int32, (TOKEN_BLOCK, TOKEN_BLOCK), 0)
    col = lax.broadcasted_iota(jnp.int32, (TOKEN_BLOCK, TOKEN_BLOCK), 1)
    tri = jnp.where(row <= col, 1.0, 0.0).astype(BF16)
    eq_rank, _ = _prefix_chunks(jnp.where(eq, 1.0, 0.0), tri)
    sel = gt | (eq & (eq_rank.astype(jnp.int32) < need))
    pos, starts = _prefix_chunks(jnp.where(sel, 1.0, 0.0), tri)
    slot_ref[...] = jnp.where(sel, pos.astype(jnp.int32), -1)
    wts_ref[...] = jnp.where(sel, aff, 0.0)
    lane = lax.broadcasted_iota(jnp.int32, (E, LANES), 1)
    cnt = jnp.full((E, LANES), cap, jnp.int32)
    for c, st in enumerate(starts):
        cnt = jnp.where(lane == c, st.astype(jnp.int32), cnt)
    cnt_ref[...] = cnt


def _route(logits_t, cap):
    B, E, S = logits_t.shape
    spec = pl.BlockSpec((None, E, S), lambda b: (b, 0, 0))
    return pl.pallas_call(
        functools.partial(_route_kernel, cap=cap),
        grid=(B,),
        in_specs=[spec],
        out_specs=[spec, spec, pl.BlockSpec((None, E, LANES), lambda b: (b, 0, 0))],
        out_shape=[jax.ShapeDtypeStruct((B, E, S), jnp.int32),
                   jax.ShapeDtypeStruct((B, E, S), F32),
                   jax.ShapeDtypeStruct((B, E, LANES), jnp.int32)],
        compiler_params=_cparams("parallel"),
        name="route",
    )(logits_t)


def _window_start(c0, cap):
    a0 = jnp.left_shift(jnp.right_shift(c0, SLOT_ALIGN_LOG2), SLOT_ALIGN_LOG2)
    return pl.multiple_of(jnp.minimum(a0, cap - SLOT_WINDOW), 1 << SLOT_ALIGN_LOG2)


def _moe_step_tokens(S):
    tb = min(S, MOE_TILES_PER_STEP * TOKEN_BLOCK)
    assert S % tb == 0
    return tb


def _gather_kernel(cnt_ref, slot_ref, wts_ref, h2_ref, xg_ref, tv_ref):
    b, step = pl.program_id(0), pl.program_id(1)
    E, cap, D = xg_ref.shape
    win = SLOT_WINDOW
    tiles = h2_ref.shape[0] // TOKEN_BLOCK

    @pl.when(step == 0)
    def _():
        xg_ref[...] = jnp.zeros(xg_ref.shape, BF16)
        tv_ref[...] = jnp.zeros(tv_ref.shape, F32)

    p_iota = lax.broadcasted_iota(jnp.int32, (win, TOKEN_BLOCK), 0)

    def tile_refs(t):
        toks = slice(t * TOKEN_BLOCK, (t + 1) * TOKEN_BLOCK)
        c = step * tiles + t
        starts = [_window_start(cnt_ref[(b * E + e) * CNT_PAD + c], cap) for e in range(E)]
        return toks, c, starts

    def add_rows(e, a, hit, rows, toks):
        sl = pl.ds(a, win)
        xg_ref[e, sl, :] = (xg_ref[e, sl, :].astype(F32) + rows).astype(BF16)
        tv_ref[e, sl, :] += jnp.sum(jnp.where(hit, wts_ref[e:e + 1, toks], 0.0), axis=1, keepdims=True)

    for t in range(tiles):
        toks, c, starts = tile_refs(t)
        hits = [(p_iota + starts[e]) == slot_ref[e:e + 1, toks] for e in range(E)]
        onehot = jnp.concatenate([jnp.where(h, 1.0, 0.0).astype(BF16) for h in hits], axis=0)
        rows = jnp.dot(onehot, h2_ref[toks, :], preferred_element_type=F32)
        for e in range(E):
            add_rows(e, starts[e], hits[e], rows[e * win:(e + 1) * win], toks)

    for t in range(tiles):
        toks, c, starts = tile_refs(t)
        for e in range(E):
            last = cnt_ref[(b * E + e) * CNT_PAD + c + 1]

            @pl.when(last > starts[e] + win)
            def _():
                def more(k, carry):
                    lo = starts[e] + k * win
                    a = pl.multiple_of(jnp.minimum(lo, cap - win), 1 << SLOT_ALIGN_LOG2)
                    hit = ((p_iota + a) == slot_ref[e:e + 1, toks]) & ((p_iota + a) >= lo)
                    add_rows(e, a, hit, jnp.dot(jnp.where(hit, 1.0, 0.0).astype(BF16), h2_ref[toks, :],
                                                preferred_element_type=F32), toks)
                    return carry

                lax.fori_loop(1, (last - starts[e] + win - 1) // win, more, 0)


def _gather(cnt_flat, slot, wts, h2, cap):
    B, E, S = slot.shape
    D = h2.shape[-1]
    assert cap % SLOT_WINDOW == 0
    tb = _moe_step_tokens(S)
    tile = lambda b, c, cnt: (b, 0, c)
    return pl.pallas_call(
        _gather_kernel,
        grid_spec=pltpu.PrefetchScalarGridSpec(
            num_scalar_prefetch=1,
            grid=(B, S // tb),
            in_specs=[pl.BlockSpec((None, E, tb), tile),
                      pl.BlockSpec((None, E, tb), tile),
                      pl.BlockSpec((None, tb, D), lambda b, c, cnt: (b, c, 0))],
            out_specs=[pl.BlockSpec((None, E, cap, D), lambda b, c, cnt: (b, 0, 0, 0)),
                       pl.BlockSpec((None, E, cap, 1), lambda b, c, cnt: (b, 0, 0, 0))]),
        out_shape=[jax.ShapeDtypeStruct((B, E, cap, D), BF16),
                   jax.ShapeDtypeStruct((B, E, cap, 1), F32)],
        compiler_params=_cparams("parallel", "arbitrary"),
        name="gather",
    )(cnt_flat, slot, wts, h2)


def _experts_kernel(xg_ref, tv_ref, wg_ref, wu_ref, wd_ref, y_ref):
    wg, wu, wd = wg_ref[...].astype(BF16), wu_ref[...].astype(BF16), wd_ref[...].astype(BF16)
    for b in range(xg_ref.shape[0]):
        xg = xg_ref[b]
        g = jnp.dot(xg, wg, preferred_element_type=F32)
        u = jnp.dot(xg, wu, preferred_element_type=F32)
        a = (g * jax.nn.sigmoid(g) * u).astype(BF16)
        y_ref[b] = (jnp.dot(a, wd, preferred_element_type=F32) * tv_ref[b]).astype(BF16)


def _experts(xg, tv, w_gate, w_up, w_down, layer):
    B, E, cap, D = xg.shape
    F = w_gate.shape[-1]
    tok = pl.BlockSpec((B, None, cap, D), lambda e: (0, e, 0, 0))
    return pl.pallas_call(
        _experts_kernel,
        grid=(E,),
        in_specs=[tok,
                  pl.BlockSpec((B, None, cap, 1), lambda e: (0, e, 0, 0)),
                  pl.BlockSpec((None, None, D, F), lambda e: (layer, e, 0, 0)),
                  pl.BlockSpec((None, None, D, F), lambda e: (layer, e, 0, 0)),
                  pl.BlockSpec((None, None, F, D), lambda e: (layer, e, 0, 0))],
        out_specs=tok,
        out_shape=jax.ShapeDtypeStruct((B, E, cap, D), BF16),
        compiler_params=_cparams("parallel"),
        name="experts",
    )(xg, tv, w_gate, w_up, w_down)


def _combine_kernel(cnt_ref, slot_ref, y_ref, x_ref, mod_ref, fg_ref, o_ref, acc_ref, *, final_norm):
    b, step = pl.program_id(0), pl.program_id(1)
    E, cap, D = y_ref.shape
    win = SLOT_WINDOW
    tiles = x_ref.shape[0] // TOKEN_BLOCK
    p_iota = lax.broadcasted_iota(jnp.int32, (win, TOKEN_BLOCK), 0)
    contract_slots = (((0,), (0,)), ((), ()))

    def tile_refs(t):
        toks = slice(t * TOKEN_BLOCK, (t + 1) * TOKEN_BLOCK)
        c = step * tiles + t
        starts = [_window_start(cnt_ref[(b * E + e) * CNT_PAD + c], cap) for e in range(E)]
        return toks, c, starts

    for t in range(tiles):
        toks, c, starts = tile_refs(t)
        ycat = jnp.concatenate([y_ref[e, pl.ds(starts[e], win), :] for e in range(E)], axis=0)
        onehot = jnp.concatenate(
            [jnp.where((p_iota + starts[e]) == slot_ref[e:e + 1, toks], 1.0, 0.0).astype(BF16)
             for e in range(E)], axis=0)
        acc_ref[toks, :] = lax.dot_general(onehot, ycat, contract_slots, preferred_element_type=F32)

    for t in range(tiles):
        toks, c, starts = tile_refs(t)
        for e in range(E):
            last = cnt_ref[(b * E + e) * CNT_PAD + c + 1]

            @pl.when(last > starts[e] + win)
            def _():
                def more(k, carry):
                    lo = starts[e] + k * win
                    a = pl.multiple_of(jnp.minimum(lo, cap - win), 1 << SLOT_ALIGN_LOG2)
                    hit = ((p_iota + a) == slot_ref[e:e + 1, toks]) & ((p_iota + a) >= lo)
                    acc_ref[toks, :] += lax.dot_general(jnp.where(hit, 1.0, 0.0).astype(BF16),
                                                        y_ref[e, pl.ds(a, win), :], contract_slots,
                                                        preferred_element_type=F32)
                    return carry

                lax.fori_loop(1, (last - starts[e] + win - 1) // win, more, 0)

    out = x_ref[...] + mod_ref[5:6, :] * acc_ref[...]
    if final_norm:
        out = _rms(out, fg_ref[...])
    o_ref[...] = out


def _combine(cnt_flat, slot, y, x, mod, layer, final_g, final_norm):
    B, S, D = x.shape
    E, cap = y.shape[1], y.shape[2]
    tb = _moe_step_tokens(S)
    tok = lambda b, c, cnt: (b, c, 0)
    return pl.pallas_call(
        functools.partial(_combine_kernel, final_norm=final_norm),
        grid_spec=pltpu.PrefetchScalarGridSpec(
            num_scalar_prefetch=1,
            grid=(B, S // tb),
            in_specs=[pl.BlockSpec((None, E, tb), lambda b, c, cnt: (b, 0, c)),
                      pl.BlockSpec((None, E, cap, D), lambda b, c, cnt: (b, 0, 0, 0)),
                      pl.BlockSpec((None, tb, D), tok),
                      _mod_spec(mod, layer),
                      pl.BlockSpec((1, D), lambda b, c, cnt: (0, 0))],
            out_specs=pl.BlockSpec((None, tb, D), tok),
            scratch_shapes=[pltpu.VMEM((tb, D), F32)]),
        out_shape=jax.ShapeDtypeStruct((B, S, D), F32),
        compiler_params=_cparams("parallel", "arbitrary"),
        name="combine",
    )(cnt_flat, slot, y, x, mod, final_g)


def _prep_weights(w_in, w_uq, w_ukv, w_oa, w_pool, w_oc, w_out, w_router, width):
    L, D, _ = w_in.shape
    o_ckv = Q_RANK
    o_kr = o_ckv + KV_RANK
    o_pu = o_kr + QK_ROPE
    o_gl = o_pu + 4 * width
    zeros = lambda *shape: jnp.zeros(shape, F32)
    w1 = jnp.concatenate([w_in[..., :o_kr], zeros(L, D, QK_NOPE), w_in[..., o_kr:o_pu],
                          zeros(L, D, HEAD_PAD - QK_NOPE - QK_ROPE), w_in[..., o_pu:o_gl]], axis=-1).astype(BF16)
    wg = w_in[..., o_gl:].astype(BF16)

    dqk = QK_NOPE + QK_ROPE
    q4 = w_uq.reshape(L, Q_RANK, N_HEADS, dqk)
    wq = jnp.concatenate([q4, zeros(L, Q_RANK, N_HEADS, HEAD_PAD - dqk)], axis=-1)
    wq = wq.reshape(L, Q_RANK, N_HEADS * HEAD_PAD).astype(BF16)

    kv4 = w_ukv.reshape(L, KV_RANK, N_HEADS, QK_NOPE + V_HEAD)
    wk = jnp.concatenate([kv4[..., :QK_NOPE], zeros(L, KV_RANK, N_HEADS, HEAD_PAD - QK_NOPE)], axis=-1)
    wkv = jnp.concatenate([wk.reshape(L, KV_RANK, N_HEADS * HEAD_PAD),
                           kv4[..., QK_NOPE:].reshape(L, KV_RANK, N_HEADS * V_HEAD)], axis=-1).astype(BF16)
    return dict(w1=w1, wg=wg, wq=wq, wkv=wkv, woa=w_oa.astype(BF16), wpool=w_pool.astype(BF16),
                woc=w_oc.astype(BF16), wout=w_out.astype(BF16), wrt=w_router.transpose(0, 2, 1))


def kernel(x, c, positions, w_mod, b_mod, norm1_g, w_in, b_gate, q_norm_g, w_uq, kv_norm_g, w_ukv, w_oa, w_pool,
           pool_scale, conv_w, w_oc, w_out, norm2_g, w_router, w_gate, w_up, w_down, final_g):
    B, S, D = x.shape
    L = w_mod.shape[0]
    E = w_router.shape[-1]
    width = w_oc.shape[1]
    cap = EC_FACTOR * S // E
    assert S % TOKEN_BLOCK == 0 and S // TOKEN_BLOCK + 1 <= CNT_PAD
    tm = min(S, 512)
    tq = min(S, 256)

    mod = _modulation(c, w_mod, b_mod).reshape(L, B, 6, D)
    ck, sk = _rope_tables(positions)
    w = _prep_weights(w_in, w_uq, w_ukv, w_oa, w_pool, w_oc, w_out, w_router, width)
    rows = lambda a: a.reshape(L, 1, -1)
    n1g, n2g, qng, kvng = rows(norm1_g), rows(norm2_g), rows(q_norm_g), rows(kv_norm_g)
    bg, pscale = rows(b_gate), rows(pool_scale)
    for l in range(L):
        q, k, v, pu, z, cb = _attn_in(x, mod, l, n1g, w["w1"], qng, w["wq"], kvng, w["wkv"], ck, sk, tm)
        o = _attention(q, k, v, tq)
        x, h2, logits_t = _mix_post(x, o, pu, z, cb, mod, l, n1g, w["wg"], bg, w["woa"], w["wpool"], pscale,
                                    conv_w, w["woc"], w["wout"], n2g, w["wrt"], tm)
        slot, wts, cnt = _route(logits_t, cap)
        cnt_flat = cnt[:, :, :CNT_PAD].reshape(-1)
        xg, tv = _gather(cnt_flat, slot, wts, h2, cap)
        y = _experts(xg, tv, w_gate, w_up, w_down, l)
        x = _combine(cnt_flat, slot, y, x, mod, l, final_g.reshape(1, D), final_norm=(l == L - 1))
    return x
```

```python
import functools

import jax
import jax.numpy as jnp
import numpy as np
from jax import lax
from jax.experimental import pallas as pl
from jax.experimental.pallas import tpu as pltpu

N_HEADS = 8
QK_NOPE = 64
QK_ROPE = 32
V_HEAD = 64
Q_RANK = 384
KV_RANK = 256
ROPE_THETA = 10000.0
POOL_WINDOWS = (2, 4, 8, 16)
POOL_GROUP = 128
CONV_K = 3
N_BRANCH = 3
N_EXPERTS = 16
EC_FACTOR = 2
EPS = 1e-6

LANES = 128
SUBLANES = 8
HEAD_PAD = 128
V_EXT = 256
HALO = 16
SLOT_WINDOW = 64
SLOT_ALIGN_LOG2 = 4
TOKEN_BLOCK = 256
MOE_TILES_PER_STEP = 2
CNT_PAD = 32
VMEM_LIMIT = 56 * 1024 * 1024

BF16 = jnp.bfloat16
F32 = jnp.float32


def _cparams(*sem):
    return pltpu.CompilerParams(dimension_semantics=sem, vmem_limit_bytes=VMEM_LIMIT)


def _layer_spec(a, layer):
    zeros = (0,) * (a.ndim - 1)
    return pl.BlockSpec((None,) + a.shape[1:], lambda *_: (layer,) + zeros)


def _mod_spec(mod, layer):
    return pl.BlockSpec((None, None) + mod.shape[2:], lambda b, *_: (layer, b, 0, 0))


def _rms(x, g):
    return x * lax.rsqrt(jnp.mean(x * x, axis=-1, keepdims=True) + EPS) * g


def _split_bf16(a):
    hi = a.astype(BF16)
    return hi, (a - hi.astype(F32)).astype(BF16)


def _mod_kernel(c_ref, w_ref, b_ref, o_ref):
    c = c_ref[...]
    c_hi, c_lo = _split_bf16(c * jax.nn.sigmoid(c))
    w_hi, w_lo = _split_bf16(w_ref[...])
    rows = c.shape[0]
    r = jnp.dot(jnp.concatenate([c_hi, c_lo], axis=0), w_hi, preferred_element_type=F32)
    out = r[:rows] + r[rows:] + jnp.dot(c_hi, w_lo, preferred_element_type=F32)
    o_ref[...] = out[:o_ref.shape[0]] + b_ref[...]


def _modulation(c, w_mod, b_mod):
    L, D, N = w_mod.shape
    B = c.shape[0]
    tn = 1536
    rows = -(-B // SUBLANES) * SUBLANES
    c = jnp.pad(c, ((0, rows - B), (0, 0)))
    return pl.pallas_call(
        _mod_kernel,
        grid=(L, N // tn),
        in_specs=[pl.BlockSpec((rows, D), lambda l, j: (0, 0)),
                  pl.BlockSpec((None, D, tn), lambda l, j: (l, 0, j)),
                  pl.BlockSpec((None, 1, tn), lambda l, j: (l, 0, j))],
        out_specs=pl.BlockSpec((None, B, tn), lambda l, j: (l, 0, j)),
        out_shape=jax.ShapeDtypeStruct((L, B, N), F32),
        compiler_params=_cparams("parallel", "parallel"),
        name="modulation",
    )(c, w_mod, b_mod.reshape(L, 1, N))


def _rope_kernel(pos_ref, freq_ref, ck_ref, sk_ref):
    ang = pos_ref[...].astype(F32) * freq_ref[...]
    lane = lax.broadcasted_iota(jnp.int32, ang.shape, 1)
    cosv, sinv = jnp.cos(ang), jnp.sin(ang)
    half = QK_ROPE // 2
    in_rope = (lane >= QK_NOPE) & (lane < QK_NOPE + QK_ROPE)
    ck_ref[...] = jnp.where(in_rope, cosv, 0.0)
    sk_ref[...] = jnp.where(in_rope, jnp.where(lane < QK_NOPE + half, -sinv, sinv), 0.0)


def _rope_tables(positions):
    B, S = positions.shape
    half = QK_ROPE // 2
    freqs = ROPE_THETA ** (-jnp.arange(0, QK_ROPE, 2, dtype=F32) / QK_ROPE)
    freq_row = jnp.concatenate([jnp.zeros((QK_NOPE,), F32), freqs, freqs,
                                jnp.zeros((HEAD_PAD - QK_NOPE - 2 * half,), F32)]).reshape(1, HEAD_PAD)
    ts = min(S, 1024)
    out = jax.ShapeDtypeStruct((B, S, HEAD_PAD), F32)
    return pl.pallas_call(
        _rope_kernel,
        grid=(B, S // ts),
        in_specs=[pl.BlockSpec((None, ts, 1), lambda b, i: (b, i, 0)),
                  pl.BlockSpec((1, HEAD_PAD), lambda b, i: (0, 0))],
        out_specs=[pl.BlockSpec((None, ts, HEAD_PAD), lambda b, i: (b, i, 0))] * 2,
        out_shape=[out, out],
        compiler_params=_cparams("parallel", "parallel"),
        name="rope_tables",
    )(positions.reshape(B, S, 1), freq_row)


def _rope_partner(x, lane_in_head):
    half = QK_ROPE // 2
    n = x.shape[1]
    from_upper = pltpu.roll(x, n - half, axis=1)
    from_lower = pltpu.roll(x, half, axis=1)
    return jnp.where(lane_in_head < QK_NOPE + half, from_upper, from_lower)


def _attn_in_kernel(x_ref, mod_ref, n1g_ref, w1_ref, qng_ref, wq_ref, kvng_ref, wkv_ref,
                    ck_ref, sk_ref,
                    q_ref, k_ref, v_ref, pu_ref, z_ref, cb_ref):
    x = x_ref[...]
    sh1, sc1 = mod_ref[0:1, :], mod_ref[1:2, :]
    h = _rms(x, n1g_ref[...]) * (1.0 + sc1) + sh1
    p = jnp.dot(h.astype(BF16), w1_ref[...], preferred_element_type=F32)
    o_ckv = Q_RANK
    o_kr = o_ckv + KV_RANK
    o_pu = o_kr + HEAD_PAD
    width = pu_ref.shape[-1]
    o_cx, o_cb, o_cc = o_pu + width, o_pu + 2 * width, o_pu + 3 * width
    cqn = _rms(p[:, 0:o_ckv], qng_ref[...]).astype(BF16)
    ckvn = _rms(p[:, o_ckv:o_kr], kvng_ref[...]).astype(BF16)
    q = jnp.dot(cqn, wq_ref[...], preferred_element_type=F32)
    kvp = jnp.dot(ckvn, wkv_ref[...], preferred_element_type=F32)
    ck, sk = ck_ref[...], sk_ref[...]
    lane = lax.broadcasted_iota(jnp.int32, ck.shape, 1)
    cq_tab = jnp.where(lane < QK_NOPE, 1.0, ck)
    scale = (QK_NOPE + QK_ROPE) ** -0.5 * float(np.log2(np.e))
    kr = p[:, o_kr:o_pu]
    krope = kr * ck + _rope_partner(kr, lane) * sk
    q_swap = _rope_partner(q, jnp.tile(lane, (1, N_HEADS)))
    for hd in range(N_HEADS):
        sl = slice(hd * HEAD_PAD, (hd + 1) * HEAD_PAD)
        q_ref[hd] = ((q[:, sl] * cq_tab + q_swap[:, sl] * sk) * scale).astype(BF16)
        k_ref[hd] = (kvp[:, sl] + krope).astype(BF16)
    vb = kvp[:, N_HEADS * HEAD_PAD:].astype(BF16)
    ones = jnp.ones((vb.shape[0], LANES), BF16)
    pieces = []
    for pr in range(N_HEADS // 2):
        pieces += [vb[:, pr * 2 * V_HEAD:(pr + 1) * 2 * V_HEAD], ones]
    v_ref[...] = jnp.concatenate(pieces, axis=1)
    pu_ref[...] = p[:, o_pu:o_cx]
    z_ref[...] = p[:, o_cc:o_cc + width] * p[:, o_cx:o_cb]
    cb_ref[...] = p[:, o_cb:o_cc]


def _attn_in(x, mod, layer, n1g, w1, qng, wq, kvng, wkv, ck, sk, tm):
    B, S, D = x.shape
    width = (w1.shape[-1] - Q_RANK - KV_RANK - HEAD_PAD) // 4
    tok = lambda b, i: (b, i, 0)
    hspec = pl.BlockSpec((None, N_HEADS, tm, HEAD_PAD), lambda b, i: (b, 0, i, 0))
    wspec = lambda a: _layer_spec(a, layer)
    f32o = jax.ShapeDtypeStruct((B, S, width), F32)
    return pl.pallas_call(
        _attn_in_kernel,
        grid=(B, S // tm),
        in_specs=[pl.BlockSpec((None, tm, D), tok),
                  _mod_spec(mod, layer),
                  wspec(n1g), wspec(w1), wspec(qng), wspec(wq), wspec(kvng), wspec(wkv),
                  pl.BlockSpec((None, tm, HEAD_PAD), tok),
                  pl.BlockSpec((None, tm, HEAD_PAD), tok)],
        out_specs=[hspec, hspec,
                   pl.BlockSpec((None, tm, N_HEADS * V_EXT // 2), tok),
                   pl.BlockSpec((None, tm, width), tok),
                   pl.BlockSpec((None, tm, width), tok),
                   pl.BlockSpec((None, tm, width), tok)],
        out_shape=[jax.ShapeDtypeStruct((B, N_HEADS, S, HEAD_PAD), BF16),
                   jax.ShapeDtypeStruct((B, N_HEADS, S, HEAD_PAD), BF16),
                   jax.ShapeDtypeStruct((B, S, N_HEADS * V_EXT // 2), BF16),
                   f32o, f32o, f32o],
        compiler_params=_cparams("parallel", "parallel"),
        name="attn_in",
    )(x, mod, n1g, w1, qng, wq, kvng, wkv, ck, sk)


def _attention_kernel(q_ref, k_ref, v_ref, o_ref):
    nh = q_ref.shape[0]
    pair = 2 * V_HEAD
    scores = [lax.dot_general(q_ref[hd], k_ref[hd], (((1,), (1,)), ((), ())),
                              preferred_element_type=F32) for hd in range(nh)]
    outs = []
    for hd in range(nh):
        s = scores[hd]
        m = jnp.max(s, axis=-1, keepdims=True)
        p = jnp.exp2(s - m).astype(BF16)
        v = v_ref[:, (hd // 2) * V_EXT:(hd // 2 + 1) * V_EXT]
        r = jnp.dot(p, v, preferred_element_type=F32)
        outs.append(r[:, :pair] / r[:, pair:pair + 1])
    lane = lax.broadcasted_iota(jnp.int32, outs[0].shape, 1)
    for pr in range(nh // 2):
        o_ref[:, pr * pair:(pr + 1) * pair] = jnp.where(lane < V_HEAD, outs[2 * pr], outs[2 * pr + 1]).astype(BF16)


def _attention(q, k, v, tq, nh=4):
    B, H, S, _ = q.shape
    return pl.pallas_call(
        _attention_kernel,
        grid=(B, H // nh, S // tq),
        in_specs=[pl.BlockSpec((None, nh, tq, HEAD_PAD), lambda b, h, i: (b, h, i, 0)),
                  pl.BlockSpec((None, nh, S, HEAD_PAD), lambda b, h, i: (b, h, 0, 0)),
                  pl.BlockSpec((None, S, nh * V_EXT // 2), lambda b, h, i: (b, 0, h))],
        out_specs=pl.BlockSpec((None, tq, nh * V_HEAD), lambda b, h, i: (b, i, h)),
        out_shape=jax.ShapeDtypeStruct((B, S, H * V_HEAD), BF16),
        compiler_params=_cparams("parallel", "parallel", "parallel"),
        name="attention",
    )(q, k, v)


def _shift_rows(ext, shift, ts):
    n = ext.shape[0]
    return pltpu.roll(ext, shift % n, axis=0)[HALO:HALO + ts]


def _mix_post_kernel(x_ref, o_ref, pu_ref, pup_ref, pun_ref, z_ref, zp_ref, zn_ref, cb_ref,
                     mod_ref, n1g_ref, wg_ref, bg_ref, woa_ref, wpool_ref, pscale_ref, convw_ref, woc_ref,
                     wout_ref, n2g_ref, wrt_ref,
                     xo_ref, h2_ref, lg_ref, *, seq_len, sub):
    i = pl.program_id(1)
    last = pl.num_programs(1) - 1
    ts, D = x_ref.shape
    sh1, sc1, g1 = mod_ref[0:1, :], mod_ref[1:2, :], mod_ref[2:3, :]
    sh2, sc2 = mod_ref[3:4, :], mod_ref[4:5, :]
    first_f = jnp.where(i == 0, 0.0, 1.0)
    last_f = jnp.where(i == last, 0.0, 1.0)
    pext = jnp.concatenate([pup_ref[...] * first_f, pu_ref[...], pun_ref[...] * last_f], axis=0)
    zext = jnp.concatenate([zp_ref[...] * first_f, z_ref[...], zn_ref[...] * last_f], axis=0)
    wr = wrt_ref[...]
    wr_hi = wr.astype(BF16)
    wr_hl = jnp.concatenate([wr_hi, (wr - wr_hi.astype(F32)).astype(BF16)], axis=0)
    n_e = wr.shape[0]
    nt_dims = (((1,), (1,)), ((), ()))

    for r0 in range(0, ts, sub):
        rows = slice(r0, r0 + sub)
        x = x_ref[rows, :]
        h = (_rms(x, n1g_ref[...]) * (1.0 + sc1) + sh1).astype(BF16)
        gl = jnp.dot(h, wg_ref[...], preferred_element_type=F32) + bg_ref[...]
        gates = jax.nn.sigmoid(gl)
        ya = jnp.dot(o_ref[rows, :], woa_ref[...], preferred_element_type=F32)

        t = i * ts + r0 + lax.broadcasted_iota(jnp.int32, (sub, 1), 0)
        pe = pext[r0:r0 + sub + 2 * HALO]
        yb_parts = []
        for gi, w in enumerate(POOL_WINDOWS):
            gs = slice(gi * POOL_GROUP, (gi + 1) * POOL_GROUP)
            a = pe[:, gs]
            span = 1
            while span < w:
                a = a + pltpu.roll(a, span, axis=0)
                span *= 2
            wsum = _shift_rows(a, -(w // 2 - 1), sub)
            lo = jnp.maximum(t - w // 2, 0)
            hi = jnp.minimum(t + w // 2 - 1, seq_len - 1)
            cnt = (hi - lo + 1).astype(F32)
            mixed = wsum / cnt - pe[HALO:HALO + sub, gs]
            yb_parts.append(jnp.dot(mixed.astype(BF16), wpool_ref[gi], preferred_element_type=F32))
        yb = jnp.concatenate(yb_parts, axis=-1) * pscale_ref[...]

        ze = zext[r0:r0 + sub + 2 * HALO]
        yconv = (convw_ref[0:1, :] * _shift_rows(ze, 1, sub) + convw_ref[1:2, :] * ze[HALO:HALO + sub]
                 + convw_ref[2:3, :] * _shift_rows(ze, -1, sub))
        yc = jnp.dot((cb_ref[rows, :] * yconv).astype(BF16), woc_ref[...], preferred_element_type=F32)

        merged = gates[:, 0:D] * ya + gates[:, D:2 * D] * yb + gates[:, 2 * D:3 * D] * yc
        xn = x + g1 * jnp.dot(merged.astype(BF16), wout_ref[...], preferred_element_type=F32)
        xo_ref[rows, :] = xn
        h2 = _rms(xn, n2g_ref[...]) * (1.0 + sc2) + sh2
        h2_hi = h2.astype(BF16)
        h2_ref[rows, :] = h2_hi
        h2_lo = (h2 - h2_hi.astype(F32)).astype(BF16)
        r_hi = lax.dot_general(wr_hl, h2_hi, nt_dims, preferred_element_type=F32)
        r_lo = lax.dot_general(wr_hi, h2_lo, nt_dims, preferred_element_type=F32)
        lg_ref[:, rows] = r_hi[:n_e] + r_hi[n_e:] + r_lo


def _mix_post(x, o, pu, z, cb, mod, layer, n1g, wg, bg, woa, wpool, pscale, convw, woc, wout, n2g, wrt, ts):
    B, S, D = x.shape
    W = pu.shape[-1]
    E = wrt.shape[1]
    nh = ts // HALO
    tok = lambda b, i: (b, i, 0)
    prev = lambda b, i: (b, jnp.maximum(i * nh - 1, 0), 0)
    nxt = lambda b, i: (b, jnp.minimum((i + 1) * nh, S // HALO - 1), 0)
    wspec = lambda a: _layer_spec(a, layer)
    return pl.pallas_call(
        functools.partial(_mix_post_kernel, seq_len=S, sub=min(ts, 256)),
        grid=(B, S // ts),
        in_specs=[pl.BlockSpec((None, ts, D), tok),
                  pl.BlockSpec((None, ts, o.shape[-1]), tok),
                  pl.BlockSpec((None, ts, W), tok),
                  pl.BlockSpec((None, HALO, W), prev),
                  pl.BlockSpec((None, HALO, W), nxt),
                  pl.BlockSpec((None, ts, W), tok),
                  pl.BlockSpec((None, HALO, W), prev),
                  pl.BlockSpec((None, HALO, W), nxt),
                  pl.BlockSpec((None, ts, W), tok),
                  _mod_spec(mod, layer),
                  wspec(n1g), wspec(wg), wspec(bg), wspec(woa), wspec(wpool), wspec(pscale), wspec(convw),
                  wspec(woc), wspec(wout), wspec(n2g), wspec(wrt)],
        out_specs=[pl.BlockSpec((None, ts, D), tok),
                   pl.BlockSpec((None, ts, D), tok),
                   pl.BlockSpec((None, E, ts), lambda b, i: (b, 0, i))],
        out_shape=[jax.ShapeDtypeStruct((B, S, D), F32),
                   jax.ShapeDtypeStruct((B, S, D), BF16),
                   jax.ShapeDtypeStruct((B, E, S), F32)],
        compiler_params=_cparams("parallel", "parallel"),
        name="mix_post",
    )(x, o, pu, pu, pu, z, z, z, cb, mod, n1g, wg, bg, woa, wpool, pscale, convw, woc, wout, n2g, wrt)


def _prefix_chunks(flags, tri):
    E, S = flags.shape
    running = jnp.zeros((E, 1), F32)
    pieces, starts = [], []
    for c in range(S // TOKEN_BLOCK):
        blk = flags[:, c * TOKEN_BLOCK:(c + 1) * TOKEN_BLOCK]
        incl = jnp.dot(blk.astype(BF16), tri, preferred_element_type=F32)
        pieces.append(incl - blk + running)
        starts.append(running)
        running = running + jnp.sum(blk, axis=-1, keepdims=True)
    starts.append(running)
    return jnp.concatenate(pieces, axis=-1), starts


def _route_kernel(lg_ref, slot_ref, wts_ref, cnt_ref, *, cap):
    lg = lg_ref[...]
    E, S = lg.shape
    m = jnp.max(lg, axis=0, keepdims=True)
    ex = jnp.exp(lg - m)
    aff = ex / jnp.sum(ex, axis=0, keepdims=True)

    def step(it, tau_bits):
        cand = tau_bits | jnp.left_shift(jnp.int32(1), 30 - it)
        n = jnp.sum((aff >= lax.bitcast_convert_type(cand, F32)).astype(jnp.int32), axis=-1, keepdims=True)
        return jnp.where(n >= cap, cand, tau_bits)

    tau = lax.bitcast_convert_type(lax.fori_loop(0, 31, step, jnp.zeros((E, 1), jnp.int32)), F32)
    gt = aff > tau
    eq = aff == tau
    need = cap - jnp.sum(gt.astype(jnp.int32), axis=-1, keepdims=True)
    row = lax.broadcasted_iota(jnp.int32, (TOKEN_BLOCK, TOKEN_BLOCK), 0)
    col = lax.broadcasted_iota(jnp.int32, (TOKEN_BLOCK, TOKEN_BLOCK), 1)
    tri = jnp.where(row <= col, 1.0, 0.0).astype(BF16)
    eq_rank, _ = _prefix_chunks(jnp.where(eq, 1.0, 0.0), tri)
    sel = gt | (eq & (eq_rank.astype(jnp.int32) < need))
    pos, starts = _prefix_chunks(jnp.where(sel, 1.0, 0.0), tri)
    slot_ref[...] = jnp.where(sel, pos.astype(jnp.int32), -1)
    wts_ref[...] = jnp.where(sel, aff, 0.0)
    lane = lax.broadcasted_iota(jnp.int32, (E, LANES), 1)
    cnt = jnp.full((E, LANES), cap, jnp.int32)
    for c, st in enumerate(starts):
        cnt = jnp.where(lane == c, st.astype(jnp.int32), cnt)
    cnt_ref[...] = cnt


def _route(logits_t, cap):
    B, E, S = logits_t.shape
    spec = pl.BlockSpec((None, E, S), lambda b: (b, 0, 0))
    return pl.pallas_call(
        functools.partial(_route_kernel, cap=cap),
        grid=(B,),
        in_specs=[spec],
        out_specs=[spec, spec, pl.BlockSpec((None, E, LANES), lambda b: (b, 0, 0))],
        out_shape=[jax.ShapeDtypeStruct((B, E, S), jnp.int32),
                   jax.ShapeDtypeStruct((B, E, S), F32),
                   jax.ShapeDtypeStruct((B, E, LANES), jnp.int32)],
        compiler_params=_cparams("parallel"),
        name="route",
    )(logits_t)


def _window_start(c0, cap):
    a0 = jnp.left_shift(jnp.right_shift(c0, SLOT_ALIGN_LOG2), SLOT_ALIGN_LOG2)
    return pl.multiple_of(jnp.minimum(a0, cap - SLOT_WINDOW), 1 << SLOT_ALIGN_LOG2)


def _moe_step_tokens(S):
    tb = min(S, MOE_TILES_PER_STEP * TOKEN_BLOCK)
    assert S % tb == 0
    return tb


def _gather_kernel(cnt_ref, slot_ref, wts_ref, h2_ref, xg_ref, tv_ref):
    b, step = pl.program_id(0), pl.program_id(1)
    E, cap, D = xg_ref.shape
    win = SLOT_WINDOW
    tiles = h2_ref.shape[0] // TOKEN_BLOCK

    @pl.when(step == 0)
    def _():
        xg_ref[...] = jnp.zeros(xg_ref.shape, BF16)
        tv_ref[...] = jnp.zeros(tv_ref.shape, F32)

    p_iota = lax.broadcasted_iota(jnp.int32, (win, TOKEN_BLOCK), 0)

    def tile_refs(t):
        toks = slice(t * TOKEN_BLOCK, (t + 1) * TOKEN_BLOCK)
        c = step * tiles + t
        starts = [_window_start(cnt_ref[(b * E + e) * CNT_PAD + c], cap) for e in range(E)]
        return toks, c, starts

    def add_rows(e, a, hit, rows, toks):
        sl = pl.ds(a, win)
        xg_ref[e, sl, :] = (xg_ref[e, sl, :].astype(F32) + rows).astype(BF16)
        tv_ref[e, sl, :] += jnp.sum(jnp.where(hit, wts_ref[e:e + 1, toks], 0.0), axis=1, keepdims=True)

    for t in range(tiles):
        toks, c, starts = tile_refs(t)
        hits = [(p_iota + starts[e]) == slot_ref[e:e + 1, toks] for e in range(E)]
        onehot = jnp.concatenate([jnp.where(h, 1.0, 0.0).astype(BF16) for h in hits], axis=0)
        rows = jnp.dot(onehot, h2_ref[toks, :], preferred_element_type=F32)
        for e in range(E):
            add_rows(e, starts[e], hits[e], rows[e * win:(e + 1) * win], toks)

    for t in range(tiles):
        toks, c, starts = tile_refs(t)
        for e in range(E):
            last = cnt_ref[(b * E + e) * CNT_PAD + c + 1]

            @pl.when(last > starts[e] + win)
            def _():
                def more(k, carry):
                    lo = starts[e] + k * win
                    a = pl.multiple_of(jnp.minimum(lo, cap - win), 1 << SLOT_ALIGN_LOG2)
                    hit = ((p_iota + a) == slot_ref[e:e + 1, toks]) & ((p_iota + a) >= lo)
                    add_rows(e, a, hit, jnp.dot(jnp.where(hit, 1.0, 0.0).astype(BF16), h2_ref[toks, :],
                                                preferred_element_type=F32), toks)
                    return carry

                lax.fori_loop(1, (last - starts[e] + win - 1) // win, more, 0)


def _gather(cnt_flat, slot, wts, h2, cap):
    B, E, S = slot.shape
    D = h2.shape[-1]
    assert cap % SLOT_WINDOW == 0
    tb = _moe_step_tokens(S)
    tile = lambda b, c, cnt: (b, 0, c)
    return pl.pallas_call(
        _gather_kernel,
        grid_spec=pltpu.PrefetchScalarGridSpec(
            num_scalar_prefetch=1,
            grid=(B, S // tb),
            in_specs=[pl.BlockSpec((None, E, tb), tile),
                      pl.BlockSpec((None, E, tb), tile),
                      pl.BlockSpec((None, tb, D), lambda b, c, cnt: (b, c, 0))],
            out_specs=[pl.BlockSpec((None, E, cap, D), lambda b, c, cnt: (b, 0, 0, 0)),
                       pl.BlockSpec((None, E, cap, 1), lambda b, c, cnt: (b, 0, 0, 0))]),
        out_shape=[jax.ShapeDtypeStruct((B, E, cap, D), BF16),
                   jax.ShapeDtypeStruct((B, E, cap, 1), F32)],
        compiler_params=_cparams("parallel", "arbitrary"),
        name="gather",
    )(cnt_flat, slot, wts, h2)


def _experts_kernel(xg_ref, tv_ref, wg_ref, wu_ref, wd_ref, y_ref):
    wg, wu, wd = wg_ref[...].astype(BF16), wu_ref[...].astype(BF16), wd_ref[...].astype(BF16)
    for b in range(xg_ref.shape[0]):
        xg = xg_ref[b]
        g = jnp.dot(xg, wg, preferred_element_type=F32)
        u = jnp.dot(xg, wu, preferred_element_type=F32)
        a = (g * jax.nn.sigmoid(g) * u).astype(BF16)
        y_ref[b] = (jnp.dot(a, wd, preferred_element_type=F32) * tv_ref[b]).astype(BF16)


def _experts(xg, tv, w_gate, w_up, w_down, layer):
    B, E, cap, D = xg.shape
    F = w_gate.shape[-1]
    tok = pl.BlockSpec((B, None, cap, D), lambda e: (0, e, 0, 0))
    return pl.pallas_call(
        _experts_kernel,
        grid=(E,),
        in_specs=[tok,
                  pl.BlockSpec((B, None, cap, 1), lambda e: (0, e, 0, 0)),
                  pl.BlockSpec((None, None, D, F), lambda e: (layer, e, 0, 0)),
                  pl.BlockSpec((None, None, D, F), lambda e: (layer, e, 0, 0)),
                  pl.BlockSpec((None, None, F, D), lambda e: (layer, e, 0, 0))],
        out_specs=tok,
        out_shape=jax.ShapeDtypeStruct((B, E, cap, D), BF16),
        compiler_params=_cparams("parallel"),
        name="experts",
    )(xg, tv, w_gate, w_up, w_down)


def _combine_kernel(cnt_ref, slot_ref, y_ref, x_ref, mod_ref, fg_ref, o_ref, acc_ref, *, final_norm):
    b, step = pl.program_id(0), pl.program_id(1)
    E, cap, D = y_ref.shape
    win = SLOT_WINDOW
    tiles = x_ref.shape[0] // TOKEN_BLOCK
    p_iota = lax.broadcasted_iota(jnp.int32, (win, TOKEN_BLOCK), 0)
    contract_slots = (((0,), (0,)), ((), ()))

    def tile_refs(t):
        toks = slice(t * TOKEN_BLOCK, (t + 1) * TOKEN_BLOCK)
        c = step * tiles + t
        starts = [_window_start(cnt_ref[(b * E + e) * CNT_PAD + c], cap) for e in range(E)]
        return toks, c, starts

    for t in range(tiles):
        toks, c, starts = tile_refs(t)
        ycat = jnp.concatenate([y_ref[e, pl.ds(starts[e], win), :] for e in range(E)], axis=0)
        onehot = jnp.concatenate(
            [jnp.where((p_iota + starts[e]) == slot_ref[e:e + 1, toks], 1.0, 0.0).astype(BF16)
             for e in range(E)], axis=0)
        acc_ref[toks, :] = lax.dot_general(onehot, ycat, contract_slots, preferred_element_type=F32)

    for t in range(tiles):
        toks, c, starts = tile_refs(t)
        for e in range(E):
            last = cnt_ref[(b * E + e) * CNT_PAD + c + 1]

            @pl.when(last > starts[e] + win)
            def _():
                def more(k, carry):
                    lo = starts[e] + k * win
                    a = pl.multiple_of(jnp.minimum(lo, cap - win), 1 << SLOT_ALIGN_LOG2)
                    hit = ((p_iota + a) == slot_ref[e:e + 1, toks]) & ((p_iota + a) >= lo)
                    acc_ref[toks, :] += lax.dot_general(jnp.where(hit, 1.0, 0.0).astype(BF16),
                                                        y_ref[e, pl.ds(a, win), :], contract_slots,
                                                        preferred_element_type=F32)
                    return carry

                lax.fori_loop(1, (last - starts[e] + win - 1) // win, more, 0)

    out = x_ref[...] + mod_ref[5:6, :] * acc_ref[...]
    if final_norm:
        out = _rms(out, fg_ref[...])
    o_ref[...] = out


def _combine(cnt_flat, slot, y, x, mod, layer, final_g, final_norm):
    B, S, D = x.shape
    E, cap = y.shape[1], y.shape[2]
    tb = _moe_step_tokens(S)
    tok = lambda b, c, cnt: (b, c, 0)
    return pl.pallas_call(
        functools.partial(_combine_kernel, final_norm=final_norm),
        grid_spec=pltpu.PrefetchScalarGridSpec(
            num_scalar_prefetch=1,
            grid=(B, S // tb),
            in_specs=[pl.BlockSpec((None, E, tb), lambda b, c, cnt: (b, 0, c)),
                      pl.BlockSpec((None, E, cap, D), lambda b, c, cnt: (b, 0, 0, 0)),
                      pl.BlockSpec((None, tb, D), tok),
                      _mod_spec(mod, layer),
                      pl.BlockSpec((1, D), lambda b, c, cnt: (0, 0))],
            out_specs=pl.BlockSpec((None, tb, D), tok),
            scratch_shapes=[pltpu.VMEM((tb, D), F32)]),
        out_shape=jax.ShapeDtypeStruct((B, S, D), F32),
        compiler_params=_cparams("parallel", "arbitrary"),
        name="combine",
    )(cnt_flat, slot, y, x, mod, final_g)


def _prep_weights(w_in, w_uq, w_ukv, w_oa, w_pool, w_oc, w_out, w_router, width):
    L, D, _ = w_in.shape
    o_ckv = Q_RANK
    o_kr = o_ckv + KV_RANK
    o_pu = o_kr + QK_ROPE
    o_gl = o_pu + 4 * width
    zeros = lambda *shape: jnp.zeros(shape, F32)
    wb = w_in.astype(BF16)
    zpad = lambda n: jnp.zeros((L, D, n), BF16)
    w1 = jnp.concatenate([wb[..., :o_kr], zpad(QK_NOPE), wb[..., o_kr:o_pu],
                          zpad(HEAD_PAD - QK_NOPE - QK_ROPE), wb[..., o_pu:o_gl]], axis=-1)
    wg = wb[..., o_gl:]

    dqk = QK_NOPE + QK_ROPE
    q4 = w_uq.reshape(L, Q_RANK, N_HEADS, dqk)
    wq = jnp.concatenate([q4, zeros(L, Q_RANK, N_HEADS, HEAD_PAD - dqk)], axis=-1)
    wq = wq.reshape(L, Q_RANK, N_HEADS * HEAD_PAD).astype(BF16)

    kv4 = w_ukv.reshape(L, KV_RANK, N_HEADS, QK_NOPE + V_HEAD)
    wk = jnp.concatenate([kv4[..., :QK_NOPE], zeros(L, KV_RANK, N_HEADS, HEAD_PAD - QK_NOPE)], axis=-1)
    wkv = jnp.concatenate([wk.reshape(L, KV_RANK, N_HEADS * HEAD_PAD),
                           kv4[..., QK_NOPE:].reshape(L, KV_RANK, N_HEADS * V_HEAD)], axis=-1).astype(BF16)
    return dict(w1=w1, wg=wg, wq=wq, wkv=wkv, woa=w_oa.astype(BF16), wpool=w_pool.astype(BF16),
                woc=w_oc.astype(BF16), wout=w_out.astype(BF16), wrt=w_router.transpose(0, 2, 1))


def kernel(x, c, positions, w_mod, b_mod, norm1_g, w_in, b_gate, q_norm_g, w_uq, kv_norm_g, w_ukv, w_oa, w_pool,
           pool_scale, conv_w, w_oc, w_out, norm2_g, w_router, w_gate, w_up, w_down, final_g):
    B, S, D = x.shape
    L = w_mod.shape[0]
    E = w_router.shape[-1]
    width = w_oc.shape[1]
    cap = EC_FACTOR * S // E
    assert S % TOKEN_BLOCK == 0 and S // TOKEN_BLOCK + 1 <= CNT_PAD
    tm = min(S, 512)
    tq = min(S, 256)

    mod = _modulation(c, w_mod, b_mod).reshape(L, B, 6, D)
    ck, sk = _rope_tables(positions)
    w = _prep_weights(w_in, w_uq, w_ukv, w_oa, w_pool, w_oc, w_out, w_router, width)
    rows = lambda a: a.reshape(L, 1, -1)
    n1g, n2g, qng, kvng = rows(norm1_g), rows(norm2_g), rows(q_norm_g), rows(kv_norm_g)
    bg, pscale = rows(b_gate), rows(pool_scale)
    for l in range(L):
        q, k, v, pu, z, cb = _attn_in(x, mod, l, n1g, w["w1"], qng, w["wq"], kvng, w["wkv"], ck, sk, tm)
        o = _attention(q, k, v, tq)
        x, h2, logits_t = _mix_post(x, o, pu, z, cb, mod, l, n1g, w["wg"], bg, w["woa"], w["wpool"], pscale,
                                    conv_w, w["woc"], w["wout"], n2g, w["wrt"], tm)
        slot, wts, cnt = _route(logits_t, cap)
        cnt_flat = cnt[:, :, :CNT_PAD].reshape(-1)
        xg, tv = _gather(cnt_flat, slot, wts, h2, cap)
        y = _experts(xg, tv, w_gate, w_up, w_down, l)
        x = _combine(cnt_flat, slot, y, x, mod, l, final_g.reshape(1, D), final_norm=(l == L - 1))
    return x
```

```python
import functools

import jax
import jax.numpy as jnp
import numpy as np
from jax import lax
from jax.experimental import pallas as pl
from jax.experimental.pallas import tpu as pltpu

N_HEADS = 8
QK_NOPE = 64
QK_ROPE = 32
V_HEAD = 64
Q_RANK = 384
KV_RANK = 256
ROPE_THETA = 10000.0
POOL_WINDOWS = (2, 4, 8, 16)
POOL_GROUP = 128
CONV_K = 3
N_BRANCH = 3
N_EXPERTS = 16
EC_FACTOR = 2
EPS = 1e-6

LANES = 128
SUBLANES = 8
HEAD_PAD = 128
V_EXT = 256
HALO = 16
SLOT_WINDOW = 64
SLOT_ALIGN_LOG2 = 4
TOKEN_BLOCK = 256
GATHER_TILES_PER_STEP = 4
COMBINE_TILES_PER_STEP = 2
CNT_PAD = 32
VMEM_LIMIT = 56 * 1024 * 1024

BF16 = jnp.bfloat16
F32 = jnp.float32


def _cparams(*sem):
    return pltpu.CompilerParams(dimension_semantics=sem, vmem_limit_bytes=VMEM_LIMIT)


def _layer_spec(a, layer):
    zeros = (0,) * (a.ndim - 1)
    return pl.BlockSpec((None,) + a.shape[1:], lambda *_: (layer,) + zeros)


def _mod_spec(mod, layer):
    return pl.BlockSpec((None, None) + mod.shape[2:], lambda b, *_: (layer, b, 0, 0))


def _rms(x, g):
    return x * lax.rsqrt(jnp.mean(x * x, axis=-1, keepdims=True) + EPS) * g


def _split_bf16(a):
    hi = a.astype(BF16)
    return hi, (a - hi.astype(F32)).astype(BF16)


def _mod_kernel(c_ref, w_ref, b_ref, o_ref):
    c = c_ref[...]
    c_hi, c_lo = _split_bf16(c * jax.nn.sigmoid(c))
    w_hi, w_lo = _split_bf16(w_ref[...])
    rows = c.shape[0]
    r = jnp.dot(jnp.concatenate([c_hi, c_lo], axis=0), w_hi, preferred_element_type=F32)
    out = r[:rows] + r[rows:] + jnp.dot(c_hi, w_lo, preferred_element_type=F32)
    o_ref[...] = out[:o_ref.shape[0]] + b_ref[...]


def _modulation(c, w_mod, b_mod):
    L, D, N = w_mod.shape
    B = c.shape[0]
    tn = 1536
    rows = -(-B // SUBLANES) * SUBLANES
    c = jnp.pad(c, ((0, rows - B), (0, 0)))
    return pl.pallas_call(
        _mod_kernel,
        grid=(L, N // tn),
        in_specs=[pl.BlockSpec((rows, D), lambda l, j: (0, 0)),
                  pl.BlockSpec((None, D, tn), lambda l, j: (l, 0, j)),
                  pl.BlockSpec((None, 1, tn), lambda l, j: (l, 0, j))],
        out_specs=pl.BlockSpec((None, B, tn), lambda l, j: (l, 0, j)),
        out_shape=jax.ShapeDtypeStruct((L, B, N), F32),
        compiler_params=_cparams("parallel", "parallel"),
        name="modulation",
    )(c, w_mod, b_mod.reshape(L, 1, N))


def _rope_kernel(pos_ref, freq_ref, ck_ref, sk_ref):
    ang = pos_ref[...].astype(F32) * freq_ref[...]
    lane = lax.broadcasted_iota(jnp.int32, ang.shape, 1)
    cosv, sinv = jnp.cos(ang), jnp.sin(ang)
    half = QK_ROPE // 2
    in_rope = (lane >= QK_NOPE) & (lane < QK_NOPE + QK_ROPE)
    ck_ref[...] = jnp.where(in_rope, cosv, 0.0)
    sk_ref[...] = jnp.where(in_rope, jnp.where(lane < QK_NOPE + half, -sinv, sinv), 0.0)


def _rope_tables(positions):
    B, S = positions.shape
    half = QK_ROPE // 2
    freqs = ROPE_THETA ** (-jnp.arange(0, QK_ROPE, 2, dtype=F32) / QK_ROPE)
    freq_row = jnp.concatenate([jnp.zeros((QK_NOPE,), F32), freqs, freqs,
                                jnp.zeros((HEAD_PAD - QK_NOPE - 2 * half,), F32)]).reshape(1, HEAD_PAD)
    ts = min(S, 1024)
    out = jax.ShapeDtypeStruct((B, S, HEAD_PAD), F32)
    return pl.pallas_call(
        _rope_kernel,
        grid=(B, S // ts),
        in_specs=[pl.BlockSpec((None, ts, 1), lambda b, i: (b, i, 0)),
                  pl.BlockSpec((1, HEAD_PAD), lambda b, i: (0, 0))],
        out_specs=[pl.BlockSpec((None, ts, HEAD_PAD), lambda b, i: (b, i, 0))] * 2,
        out_shape=[out, out],
        compiler_params=_cparams("parallel", "parallel"),
        name="rope_tables",
    )(positions.reshape(B, S, 1), freq_row)


def _rope_partner(x, lane_in_head):
    half = QK_ROPE // 2
    n = x.shape[1]
    from_upper = pltpu.roll(x, n - half, axis=1)
    from_lower = pltpu.roll(x, half, axis=1)
    return jnp.where(lane_in_head < QK_NOPE + half, from_upper, from_lower)


def _attn_in_kernel(x_ref, mod_ref, n1g_ref, w1_ref, qng_ref, wq_ref, kvng_ref, wkv_ref,
                    ck_ref, sk_ref,
                    q_ref, k_ref, v_ref, pu_ref, z_ref, cb_ref):
    x = x_ref[...]
    sh1, sc1 = mod_ref[0:1, :], mod_ref[1:2, :]
    h = _rms(x, n1g_ref[...]) * (1.0 + sc1) + sh1
    p = jnp.dot(h.astype(BF16), w1_ref[...], preferred_element_type=F32)
    o_ckv = Q_RANK
    o_kr = o_ckv + KV_RANK
    o_pu = o_kr + HEAD_PAD
    width = pu_ref.shape[-1]
    o_cx, o_cb, o_cc = o_pu + width, o_pu + 2 * width, o_pu + 3 * width
    cqn = _rms(p[:, 0:o_ckv], qng_ref[...]).astype(BF16)
    ckvn = _rms(p[:, o_ckv:o_kr], kvng_ref[...]).astype(BF16)
    q = jnp.dot(cqn, wq_ref[...], preferred_element_type=F32)
    kvp = jnp.dot(ckvn, wkv_ref[...], preferred_element_type=F32)
    ck, sk = ck_ref[...], sk_ref[...]
    lane = lax.broadcasted_iota(jnp.int32, ck.shape, 1)
    cq_tab = jnp.where(lane < QK_NOPE, 1.0, ck)
    scale = (QK_NOPE + QK_ROPE) ** -0.5 * float(np.log2(np.e))
    kr = p[:, o_kr:o_pu]
    krope = kr * ck + _rope_partner(kr, lane) * sk
    q_swap = _rope_partner(q, jnp.tile(lane, (1, N_HEADS)))
    for hd in range(N_HEADS):
        sl = slice(hd * HEAD_PAD, (hd + 1) * HEAD_PAD)
        q_ref[hd] = ((q[:, sl] * cq_tab + q_swap[:, sl] * sk) * scale).astype(BF16)
        k_ref[hd] = (kvp[:, sl] + krope).astype(BF16)
    vb = kvp[:, N_HEADS * HEAD_PAD:].astype(BF16)
    ones = jnp.ones((vb.shape[0], LANES), BF16)
    pieces = []
    for pr in range(N_HEADS // 2):
        pieces += [vb[:, pr * 2 * V_HEAD:(pr + 1) * 2 * V_HEAD], ones]
    v_ref[...] = jnp.concatenate(pieces, axis=1)
    pu_ref[...] = p[:, o_pu:o_cx]
    z_ref[...] = p[:, o_cc:o_cc + width] * p[:, o_cx:o_cb]
    cb_ref[...] = p[:, o_cb:o_cc]


def _attn_in(x, mod, layer, n1g, w1, qng, wq, kvng, wkv, ck, sk, tm):
    B, S, D = x.shape
    width = (w1.shape[-1] - Q_RANK - KV_RANK - HEAD_PAD) // 4
    tok = lambda b, i: (b, i, 0)
    hspec = pl.BlockSpec((None, N_HEADS, tm, HEAD_PAD), lambda b, i: (b, 0, i, 0))
    wspec = lambda a: _layer_spec(a, layer)
    f32o = jax.ShapeDtypeStruct((B, S, width), F32)
    return pl.pallas_call(
        _attn_in_kernel,
        grid=(B, S // tm),
        in_specs=[pl.BlockSpec((None, tm, D), tok),
                  _mod_spec(mod, layer),
                  wspec(n1g), wspec(w1), wspec(qng), wspec(wq), wspec(kvng), wspec(wkv),
                  pl.BlockSpec((None, tm, HEAD_PAD), tok),
                  pl.BlockSpec((None, tm, HEAD_PAD), tok)],
        out_specs=[hspec, hspec,
                   pl.BlockSpec((None, tm, N_HEADS * V_EXT // 2), tok),
                   pl.BlockSpec((None, tm, width), tok),
                   pl.BlockSpec((None, tm, width), tok),
                   pl.BlockSpec((None, tm, width), tok)],
        out_shape=[jax.ShapeDtypeStruct((B, N_HEADS, S, HEAD_PAD), BF16),
                   jax.ShapeDtypeStruct((B, N_HEADS, S, HEAD_PAD), BF16),
                   jax.ShapeDtypeStruct((B, S, N_HEADS * V_EXT // 2), BF16),
                   f32o, f32o, f32o],
        compiler_params=_cparams("parallel", "parallel"),
        name="attn_in",
    )(x, mod, n1g, w1, qng, wq, kvng, wkv, ck, sk)


def _attention_kernel(q_ref, k_ref, v_ref, o_ref):
    nh = q_ref.shape[0]
    pair = 2 * V_HEAD
    scores = [lax.dot_general(q_ref[hd], k_ref[hd], (((1,), (1,)), ((), ())),
                              preferred_element_type=F32) for hd in range(nh)]
    outs = []
    for hd in range(nh):
        s = scores[hd]
        m = jnp.max(s, axis=-1, keepdims=True)
        p = jnp.exp2(s - m).astype(BF16)
        v = v_ref[:, (hd // 2) * V_EXT:(hd // 2 + 1) * V_EXT]
        r = jnp.dot(p, v, preferred_element_type=F32)
        outs.append(r[:, :pair] / r[:, pair:pair + 1])
    lane = lax.broadcasted_iota(jnp.int32, outs[0].shape, 1)
    for pr in range(nh // 2):
        o_ref[:, pr * pair:(pr + 1) * pair] = jnp.where(lane < V_HEAD, outs[2 * pr], outs[2 * pr + 1]).astype(BF16)


def _attention(q, k, v, tq, nh=4):
    B, H, S, _ = q.shape
    return pl.pallas_call(
        _attention_kernel,
        grid=(B, H // nh, S // tq),
        in_specs=[pl.BlockSpec((None, nh, tq, HEAD_PAD), lambda b, h, i: (b, h, i, 0)),
                  pl.BlockSpec((None, nh, S, HEAD_PAD), lambda b, h, i: (b, h, 0, 0)),
                  pl.BlockSpec((None, S, nh * V_EXT // 2), lambda b, h, i: (b, 0, h))],
        out_specs=pl.BlockSpec((None, tq, nh * V_HEAD), lambda b, h, i: (b, i, h)),
        out_shape=jax.ShapeDtypeStruct((B, S, H * V_HEAD), BF16),
        compiler_params=_cparams("parallel", "parallel", "parallel"),
        name="attention",
    )(q, k, v)


def _shift_rows(ext, shift, ts):
    n = ext.shape[0]
    return pltpu.roll(ext, shift % n, axis=0)[HALO:HALO + ts]


def _mix_post_kernel(x_ref, o_ref, pu_ref, pup_ref, pun_ref, z_ref, zp_ref, zn_ref, cb_ref,
                     mod_ref, n1g_ref, wg_ref, bg_ref, woa_ref, wpool_ref, pscale_ref, convw_ref, woc_ref,
                     wout_ref, n2g_ref, wrt_ref,
                     xo_ref, h2_ref, lg_ref, *, seq_len, sub):
    i = pl.program_id(1)
    last = pl.num_programs(1) - 1
    ts, D = x_ref.shape
    sh1, sc1, g1 = mod_ref[0:1, :], mod_ref[1:2, :], mod_ref[2:3, :]
    sh2, sc2 = mod_ref[3:4, :], mod_ref[4:5, :]
    first_f = jnp.where(i == 0, 0.0, 1.0)
    last_f = jnp.where(i == last, 0.0, 1.0)
    pext = jnp.concatenate([pup_ref[...] * first_f, pu_ref[...], pun_ref[...] * last_f], axis=0)
    zext = jnp.concatenate([zp_ref[...] * first_f, z_ref[...], zn_ref[...] * last_f], axis=0)
    wr = wrt_ref[...]
    wr_hi = wr.astype(BF16)
    wr_hl = jnp.concatenate([wr_hi, (wr - wr_hi.astype(F32)).astype(BF16)], axis=0)
    n_e = wr.shape[0]
    nt_dims = (((1,), (1,)), ((), ()))

    for r0 in range(0, ts, sub):
        rows = slice(r0, r0 + sub)
        x = x_ref[rows, :]
        h = (_rms(x, n1g_ref[...]) * (1.0 + sc1) + sh1).astype(BF16)
        gl = jnp.dot(h, wg_ref[...], preferred_element_type=F32) + bg_ref[...]
        gates = jax.nn.sigmoid(gl)
        ya = jnp.dot(o_ref[rows, :], woa_ref[...], preferred_element_type=F32)

        t = i * ts + r0 + lax.broadcasted_iota(jnp.int32, (sub, 1), 0)
        pe = pext[r0:r0 + sub + 2 * HALO]
        yb_parts = []
        for gi, w in enumerate(POOL_WINDOWS):
            gs = slice(gi * POOL_GROUP, (gi + 1) * POOL_GROUP)
            a = pe[:, gs]
            span = 1
            while span < w:
                a = a + pltpu.roll(a, span, axis=0)
                span *= 2
            wsum = _shift_rows(a, -(w // 2 - 1), sub)
            lo = jnp.maximum(t - w // 2, 0)
            hi = jnp.minimum(t + w // 2 - 1, seq_len - 1)
            cnt = (hi - lo + 1).astype(F32)
            mixed = wsum / cnt - pe[HALO:HALO + sub, gs]
            yb_parts.append(jnp.dot(mixed.astype(BF16), wpool_ref[gi], preferred_element_type=F32))
        yb = jnp.concatenate(yb_parts, axis=-1) * pscale_ref[...]

        ze = zext[r0:r0 + sub + 2 * HALO]
        yconv = (convw_ref[0:1, :] * _shift_rows(ze, 1, sub) + convw_ref[1:2, :] * ze[HALO:HALO + sub]
                 + convw_ref[2:3, :] * _shift_rows(ze, -1, sub))
        yc = jnp.dot((cb_ref[rows, :] * yconv).astype(BF16), woc_ref[...], preferred_element_type=F32)

        merged = gates[:, 0:D] * ya + gates[:, D:2 * D] * yb + gates[:, 2 * D:3 * D] * yc
        xn = x + g1 * jnp.dot(merged.astype(BF16), wout_ref[...], preferred_element_type=F32)
        xo_ref[rows, :] = xn
        h2 = _rms(xn, n2g_ref[...]) * (1.0 + sc2) + sh2
        h2_hi = h2.astype(BF16)
        h2_ref[rows, :] = h2_hi
        h2_lo = (h2 - h2_hi.astype(F32)).astype(BF16)
        r_hi = lax.dot_general(wr_hl, h2_hi, nt_dims, preferred_element_type=F32)
        r_lo = lax.dot_general(wr_hi, h2_lo, nt_dims, preferred_element_type=F32)
        lg_ref[:, rows] = r_hi[:n_e] + r_hi[n_e:] + r_lo


def _mix_post(x, o, pu, z, cb, mod, layer, n1g, wg, bg, woa, wpool, pscale, convw, woc, wout, n2g, wrt, ts):
    B, S, D = x.shape
    W = pu.shape[-1]
    E = wrt.shape[1]
    nh = ts // HALO
    tok = lambda b, i: (b, i, 0)
    prev = lambda b, i: (b, jnp.maximum(i * nh - 1, 0), 0)
    nxt = lambda b, i: (b, jnp.minimum((i + 1) * nh, S // HALO - 1), 0)
    wspec = lambda a: _layer_spec(a, layer)
    return pl.pallas_call(
        functools.partial(_mix_post_kernel, seq_len=S, sub=ts),
        grid=(B, S // ts),
        in_specs=[pl.BlockSpec((None, ts, D), tok),
                  pl.BlockSpec((None, ts, o.shape[-1]), tok),
                  pl.BlockSpec((None, ts, W), tok),
                  pl.BlockSpec((None, HALO, W), prev),
                  pl.BlockSpec((None, HALO, W), nxt),
                  pl.BlockSpec((None, ts, W), tok),
                  pl.BlockSpec((None, HALO, W), prev),
                  pl.BlockSpec((None, HALO, W), nxt),
                  pl.BlockSpec((None, ts, W), tok),
                  _mod_spec(mod, layer),
                  wspec(n1g), wspec(wg), wspec(bg), wspec(woa), wspec(wpool), wspec(pscale), wspec(convw),
                  wspec(woc), wspec(wout), wspec(n2g), wspec(wrt)],
        out_specs=[pl.BlockSpec((None, ts, D), tok),
                   pl.BlockSpec((None, ts, D), tok),
                   pl.BlockSpec((None, E, ts), lambda b, i: (b, 0, i))],
        out_shape=[jax.ShapeDtypeStruct((B, S, D), F32),
                   jax.ShapeDtypeStruct((B, S, D), BF16),
                   jax.ShapeDtypeStruct((B, E, S), F32)],
        compiler_params=_cparams("parallel", "parallel"),
        name="mix_post",
    )(x, o, pu, pu, pu, z, z, z, cb, mod, n1g, wg, bg, woa, wpool, pscale, convw, woc, wout, n2g, wrt)


def _prefix_chunks(flags, tri):
    E, S = flags.shape
    running = jnp.zeros((E, 1), F32)
    pieces, starts = [], []
    for c in range(S // TOKEN_BLOCK):
        blk = flags[:, c * TOKEN_BLOCK:(c + 1) * TOKEN_BLOCK]
        incl = jnp.dot(blk.astype(BF16), tri, preferred_element_type=F32)
        pieces.append(incl - blk + running)
        starts.append(running)
        running = running + jnp.sum(blk, axis=-1, keepdims=True)
    starts.append(running)
    return jnp.concatenate(pieces, axis=-1), starts


def _route_kernel(lg_ref, slot_ref, wts_ref, cnt_ref, *, cap):
    lg = lg_ref[...]
    B, n_e, S = lg.shape
    m = jnp.max(lg, axis=1, keepdims=True)
    ex = jnp.exp(lg - m)
    aff = (ex / jnp.sum(ex, axis=1, keepdims=True)).reshape(B * n_e, S)
    E = B * n_e

    def step(it, tau_bits):
        cand = tau_bits | jnp.left_shift(jnp.int32(1), 30 - it)
        n = jnp.sum((aff >= lax.bitcast_convert_type(cand, F32)).astype(jnp.int32), axis=-1, keepdims=True)
        return jnp.where(n >= cap, cand, tau_bits)

    tau = lax.bitcast_convert_type(lax.fori_loop(0, 31, step, jnp.zeros((E, 1), jnp.int32)), F32)
    gt = aff > tau
    eq = aff == tau
    need = cap - jnp.sum(gt.astype(jnp.int32), axis=-1, keepdims=True)
    row = lax.broadcasted_iota(jnp.int32, (TOKEN_BLOCK, TOKEN_BLOCK), 0)
    col = lax.broadcasted_iota(jnp.int32, (TOKEN_BLOCK, TOKEN_BLOCK), 1)
    tri = jnp.where(row <= col, 1.0, 0.0).astype(BF16)
    eq_rank, _ = _prefix_chunks(jnp.where(eq, 1.0, 0.0), tri)
    sel = gt | (eq & (eq_rank.astype(jnp.int32) < need))
    pos, starts = _prefix_chunks(jnp.where(sel, 1.0, 0.0), tri)
    slot_ref[...] = jnp.where(sel, pos.astype(jnp.int32), -1).reshape(B, n_e, S)
    wts_ref[...] = jnp.where(sel, aff, 0.0).reshape(B, n_e, S)
    lane = lax.broadcasted_iota(jnp.int32, (E, LANES), 1)
    cnt = jnp.full((E, LANES), cap, jnp.int32)
    for c, st in enumerate(starts):
        cnt = jnp.where(lane == c, st.astype(jnp.int32), cnt)
    cnt_ref[...] = cnt.reshape(B, n_e, LANES)


def _route(logits_t, cap):
    B, E, S = logits_t.shape
    spec = pl.BlockSpec((B, E, S), lambda i: (0, 0, 0))
    return pl.pallas_call(
        functools.partial(_route_kernel, cap=cap),
        grid=(1,),
        in_specs=[spec],
        out_specs=[spec, spec, pl.BlockSpec((B, E, LANES), lambda i: (0, 0, 0))],
        out_shape=[jax.ShapeDtypeStruct((B, E, S), jnp.int32),
                   jax.ShapeDtypeStruct((B, E, S), F32),
                   jax.ShapeDtypeStruct((B, E, LANES), jnp.int32)],
        compiler_params=_cparams("arbitrary"),
        name="route",
    )(logits_t)


def _window_start(c0, cap):
    a0 = jnp.left_shift(jnp.right_shift(c0, SLOT_ALIGN_LOG2), SLOT_ALIGN_LOG2)
    return pl.multiple_of(jnp.minimum(a0, cap - SLOT_WINDOW), 1 << SLOT_ALIGN_LOG2)


def _moe_step_tokens(S, tiles):
    tb = min(S, tiles * TOKEN_BLOCK)
    assert S % tb == 0
    return tb


def _gather_kernel(cnt_ref, slot_ref, wts_ref, h2_ref, xg_ref, tv_ref):
    b, step = pl.program_id(0), pl.program_id(1)
    E, cap, D = xg_ref.shape
    win = SLOT_WINDOW
    tiles = h2_ref.shape[0] // TOKEN_BLOCK

    @pl.when(step == 0)
    def _():
        xg_ref[...] = jnp.zeros(xg_ref.shape, BF16)
        tv_ref[...] = jnp.zeros(tv_ref.shape, F32)

    p_iota = lax.broadcasted_iota(jnp.int32, (win, TOKEN_BLOCK), 0)

    def tile_refs(t):
        toks = slice(t * TOKEN_BLOCK, (t + 1) * TOKEN_BLOCK)
        c = step * tiles + t
        starts = [_window_start(cnt_ref[(b * E + e) * CNT_PAD + c], cap) for e in range(E)]
        return toks, c, starts

    def add_rows(e, a, hit, rows, toks):
        sl = pl.ds(a, win)
        xg_ref[e, sl, :] = (xg_ref[e, sl, :].astype(F32) + rows).astype(BF16)
        tv_ref[e, sl, :] += jnp.sum(jnp.where(hit, wts_ref[e:e + 1, toks], 0.0), axis=1, keepdims=True)

    for t in range(tiles):
        toks, c, starts = tile_refs(t)
        hits = [(p_iota + starts[e]) == slot_ref[e:e + 1, toks] for e in range(E)]
        onehot = jnp.concatenate([jnp.where(h, 1.0, 0.0).astype(BF16) for h in hits], axis=0)
        rows = jnp.dot(onehot, h2_ref[toks, :], preferred_element_type=F32)
        for e in range(E):
            add_rows(e, starts[e], hits[e], rows[e * win:(e + 1) * win], toks)

    for t in range(tiles):
        toks, c, starts = tile_refs(t)
        for e in range(E):
            last = cnt_ref[(b * E + e) * CNT_PAD + c + 1]

            @pl.when(last > starts[e] + win)
            def _():
                def more(k, carry):
                    lo = starts[e] + k * win
                    a = pl.multiple_of(jnp.minimum(lo, cap - win), 1 << SLOT_ALIGN_LOG2)
                    hit = ((p_iota + a) == slot_ref[e:e + 1, toks]) & ((p_iota + a) >= lo)
                    add_rows(e, a, hit, jnp.dot(jnp.where(hit, 1.0, 0.0).astype(BF16), h2_ref[toks, :],
                                                preferred_element_type=F32), toks)
                    return carry

                lax.fori_loop(1, (last - starts[e] + win - 1) // win, more, 0)


def _gather(cnt_flat, slot, wts, h2, cap):
    B, E, S = slot.shape
    D = h2.shape[-1]
    assert cap % SLOT_WINDOW == 0
    tb = _moe_step_tokens(S, GATHER_TILES_PER_STEP)
    tile = lambda b, c, cnt: (b, 0, c)
    return pl.pallas_call(
        _gather_kernel,
        grid_spec=pltpu.PrefetchScalarGridSpec(
            num_scalar_prefetch=1,
            grid=(B, S // tb),
            in_specs=[pl.BlockSpec((None, E, tb), tile),
                      pl.BlockSpec((None, E, tb), tile),
                      pl.BlockSpec((None, tb, D), lambda b, c, cnt: (b, c, 0))],
            out_specs=[pl.BlockSpec((None, E, cap, D), lambda b, c, cnt: (b, 0, 0, 0)),
                       pl.BlockSpec((None, E, cap, 1), lambda b, c, cnt: (b, 0, 0, 0))]),
        out_shape=[jax.ShapeDtypeStruct((B, E, cap, D), BF16),
                   jax.ShapeDtypeStruct((B, E, cap, 1), F32)],
        compiler_params=_cparams("parallel", "arbitrary"),
        name="gather",
    )(cnt_flat, slot, wts, h2)


def _experts_kernel(xg_ref, tv_ref, wg_ref, wu_ref, wd_ref, y_ref):
    wg, wu, wd = wg_ref[...].astype(BF16), wu_ref[...].astype(BF16), wd_ref[...].astype(BF16)
    for b in range(xg_ref.shape[0]):
        xg = xg_ref[b]
        g = jnp.dot(xg, wg, preferred_element_type=F32)
        u = jnp.dot(xg, wu, preferred_element_type=F32)
        a = (g * jax.nn.sigmoid(g) * u).astype(BF16)
        y_ref[b] = (jnp.dot(a, wd, preferred_element_type=F32) * tv_ref[b]).astype(BF16)


def _experts(xg, tv, w_gate, w_up, w_down, layer):
    B, E, cap, D = xg.shape
    F = w_gate.shape[-1]
    tok = pl.BlockSpec((B, None, cap, D), lambda e: (0, e, 0, 0))
    return pl.pallas_call(
        _experts_kernel,
        grid=(E,),
        in_specs=[tok,
                  pl.BlockSpec((B, None, cap, 1), lambda e: (0, e, 0, 0)),
                  pl.BlockSpec((None, None, D, F), lambda e: (layer, e, 0, 0)),
                  pl.BlockSpec((None, None, D, F), lambda e: (layer, e, 0, 0)),
                  pl.BlockSpec((None, None, F, D), lambda e: (layer, e, 0, 0))],
        out_specs=tok,
        out_shape=jax.ShapeDtypeStruct((B, E, cap, D), BF16),
        compiler_params=_cparams("parallel"),
        name="experts",
    )(xg, tv, w_gate, w_up, w_down)


def _combine_kernel(cnt_ref, slot_ref, y_ref, x_ref, mod_ref, fg_ref, o_ref, acc_ref, *, final_norm):
    b, step = pl.program_id(0), pl.program_id(1)
    E, cap, D = y_ref.shape
    win = SLOT_WINDOW
    tiles = x_ref.shape[0] // TOKEN_BLOCK
    p_iota = lax.broadcasted_iota(jnp.int32, (win, TOKEN_BLOCK), 0)
    contract_slots = (((0,), (0,)), ((), ()))

    def tile_refs(t):
        toks = slice(t * TOKEN_BLOCK, (t + 1) * TOKEN_BLOCK)
        c = step * tiles + t
        starts = [_window_start(cnt_ref[(b * E + e) * CNT_PAD + c], cap) for e in range(E)]
        return toks, c, starts

    for t in range(tiles):
        toks, c, starts = tile_refs(t)
        ycat = jnp.concatenate([y_ref[e, pl.ds(starts[e], win), :] for e in range(E)], axis=0)
        onehot = jnp.concatenate(
            [jnp.where((p_iota + starts[e]) == slot_ref[e:e + 1, toks], 1.0, 0.0).astype(BF16)
             for e in range(E)], axis=0)
        acc_ref[toks, :] = lax.dot_general(onehot, ycat, contract_slots, preferred_element_type=F32)

    for t in range(tiles):
        toks, c, starts = tile_refs(t)
        for e in range(E):
            last = cnt_ref[(b * E + e) * CNT_PAD + c + 1]

            @pl.when(last > starts[e] + win)
            def _():
                def more(k, carry):
                    lo = starts[e] + k * win
                    a = pl.multiple_of(jnp.minimum(lo, cap - win), 1 << SLOT_ALIGN_LOG2)
                    hit = ((p_iota + a) == slot_ref[e:e + 1, toks]) & ((p_iota + a) >= lo)
                    acc_ref[toks, :] += lax.dot_general(jnp.where(hit, 1.0, 0.0).astype(BF16),
                                                        y_ref[e, pl.ds(a, win), :], contract_slots,
                                                        preferred_element_type=F32)
                    return carry

                lax.fori_loop(1, (last - starts[e] + win - 1) // win, more, 0)

    out = x_ref[...] + mod_ref[5:6, :] * acc_ref[...]
    if final_norm:
        out = _rms(out, fg_ref[...])
    o_ref[...] = out


def _combine(cnt_flat, slot, y, x, mod, layer, final_g, final_norm):
    B, S, D = x.shape
    E, cap = y.shape[1], y.shape[2]
    tb = _moe_step_tokens(S, COMBINE_TILES_PER_STEP)
    tok = lambda b, c, cnt: (b, c, 0)
    return pl.pallas_call(
        functools.partial(_combine_kernel, final_norm=final_norm),
        grid_spec=pltpu.PrefetchScalarGridSpec(
            num_scalar_prefetch=1,
            grid=(B, S // tb),
            in_specs=[pl.BlockSpec((None, E, tb), lambda b, c, cnt: (b, 0, c)),
                      pl.BlockSpec((None, E, cap, D), lambda b, c, cnt: (b, 0, 0, 0)),
                      pl.BlockSpec((None, tb, D), tok),
                      _mod_spec(mod, layer),
                      pl.BlockSpec((1, D), lambda b, c, cnt: (0, 0))],
            out_specs=pl.BlockSpec((None, tb, D), tok),
            scratch_shapes=[pltpu.VMEM((tb, D), F32)]),
        out_shape=jax.ShapeDtypeStruct((B, S, D), F32),
        compiler_params=_cparams("parallel", "arbitrary"),
        name="combine",
    )(cnt_flat, slot, y, x, mod, final_g)


def _prep_weights(w_in, w_uq, w_ukv, w_oa, w_pool, w_oc, w_out, w_router, width):
    L, D, _ = w_in.shape
    o_ckv = Q_RANK
    o_kr = o_ckv + KV_RANK
    o_pu = o_kr + QK_ROPE
    o_gl = o_pu + 4 * width
    zeros = lambda *shape: jnp.zeros(shape, F32)
    wb = w_in.astype(BF16)
    zpad = lambda n: jnp.zeros((L, D, n), BF16)
    w1 = jnp.concatenate([wb[..., :o_kr], zpad(QK_NOPE), wb[..., o_kr:o_pu],
                          zpad(HEAD_PAD - QK_NOPE - QK_ROPE), wb[..., o_pu:o_gl]], axis=-1)
    wg = wb[..., o_gl:]

    dqk = QK_NOPE + QK_ROPE
    q4 = w_uq.reshape(L, Q_RANK, N_HEADS, dqk)
    wq = jnp.concatenate([q4, zeros(L, Q_RANK, N_HEADS, HEAD_PAD - dqk)], axis=-1)
    wq = wq.reshape(L, Q_RANK, N_HEADS * HEAD_PAD).astype(BF16)

    kv4 = w_ukv.reshape(L, KV_RANK, N_HEADS, QK_NOPE + V_HEAD)
    wk = jnp.concatenate([kv4[..., :QK_NOPE], zeros(L, KV_RANK, N_HEADS, HEAD_PAD - QK_NOPE)], axis=-1)
    wkv = jnp.concatenate([wk.reshape(L, KV_RANK, N_HEADS * HEAD_PAD),
                           kv4[..., QK_NOPE:].reshape(L, KV_RANK, N_HEADS * V_HEAD)], axis=-1).astype(BF16)
    return dict(w1=w1, wg=wg, wq=wq, wkv=wkv, woa=w_oa.astype(BF16), wpool=w_pool.astype(BF16),
                woc=w_oc.astype(BF16), wout=w_out.astype(BF16), wrt=w_router.transpose(0, 2, 1))


def kernel(x, c, positions, w_mod, b_mod, norm1_g, w_in, b_gate, q_norm_g, w_uq, kv_norm_g, w_ukv, w_oa, w_pool,
           pool_scale, conv_w, w_oc, w_out, norm2_g, w_router, w_gate, w_up, w_down, final_g):
    B, S, D = x.shape
    L = w_mod.shape[0]
    E = w_router.shape[-1]
    width = w_oc.shape[1]
    cap = EC_FACTOR * S // E
    assert S % TOKEN_BLOCK == 0 and S // TOKEN_BLOCK + 1 <= CNT_PAD
    tm = min(S, 512)
    tq = min(S, 256)

    mod = _modulation(c, w_mod, b_mod).reshape(L, B, 6, D)
    ck, sk = _rope_tables(positions)
    w = _prep_weights(w_in, w_uq, w_ukv, w_oa, w_pool, w_oc, w_out, w_router, width)
    rows = lambda a: a.reshape(L, 1, -1)
    n1g, n2g, qng, kvng = rows(norm1_g), rows(norm2_g), rows(q_norm_g), rows(kv_norm_g)
    bg, pscale = rows(b_gate), rows(pool_scale)
    for l in range(L):
        q, k, v, pu, z, cb = _attn_in(x, mod, l, n1g, w["w1"], qng, w["wq"], kvng, w["wkv"], ck, sk, tm)
        o = _attention(q, k, v, tq)
        x, h2, logits_t = _mix_post(x, o, pu, z, cb, mod, l, n1g, w["wg"], bg, w["woa"], w["wpool"], pscale,
                                    conv_w, w["woc"], w["wout"], n2g, w["wrt"], tm)
        slot, wts, cnt = _route(logits_t, cap)
        cnt_flat = cnt[:, :, :CNT_PAD].reshape(-1)
        xg, tv = _gather(cnt_flat, slot, wts, h2, cap)
        y = _experts(xg, tv, w_gate, w_up, w_down, l)
        x = _combine(cnt_flat, slot, y, x, mod, l, final_g.reshape(1, D), final_norm=(l == L - 1))
    return x
```

```python
import functools

import jax
import jax.numpy as jnp
import numpy as np
from jax import lax
from jax.experimental import pallas as pl
from jax.experimental.pallas import tpu as pltpu

N_HEADS = 8
QK_NOPE = 64
QK_ROPE = 32
V_HEAD = 64
Q_RANK = 384
KV_RANK = 256
ROPE_THETA = 10000.0
POOL_WINDOWS = (2, 4, 8, 16)
POOL_GROUP = 128
EC_FACTOR = 2
EPS = 1e-6

LANES = 128
SUBLANES = 8
HEAD_PAD = 128
V_EXT = 256
HALO = 16
SLOT_WINDOW = 64
SLOT_ALIGN_LOG2 = 4
TOKEN_BLOCK = 256
GATHER_TILES_PER_STEP = 4
COMBINE_TILES_PER_STEP = 2
CNT_PAD = 32
VMEM_LIMIT = 56 * 1024 * 1024

BF16 = jnp.bfloat16
F32 = jnp.float32


def _cparams(*sem):
    return pltpu.CompilerParams(dimension_semantics=sem, vmem_limit_bytes=VMEM_LIMIT)


def _layer_spec(a, layer):
    zeros = (0,) * (a.ndim - 1)
    return pl.BlockSpec((None,) + a.shape[1:], lambda *_: (layer,) + zeros)


def _mod_spec(mod, layer):
    return pl.BlockSpec((None, None) + mod.shape[2:], lambda b, *_: (layer, b, 0, 0))


def _rms(x, g):
    return x * lax.rsqrt(jnp.mean(x * x, axis=-1, keepdims=True) + EPS) * g


def _split_bf16(a):
    hi = a.astype(BF16)
    return hi, (a - hi.astype(F32)).astype(BF16)


def _mod_kernel(c_ref, w_ref, b_ref, o_ref):
    c = c_ref[...]
    c_hi, c_lo = _split_bf16(c * jax.nn.sigmoid(c))
    w_hi, w_lo = _split_bf16(w_ref[...])
    rows = c.shape[0]
    r = jnp.dot(jnp.concatenate([c_hi, c_lo], axis=0), w_hi, preferred_element_type=F32)
    out = r[:rows] + r[rows:] + jnp.dot(c_hi, w_lo, preferred_element_type=F32)
    o_ref[...] = out[:o_ref.shape[0]] + b_ref[...]


def _modulation(c, w_mod, b_mod):
    L, D, N = w_mod.shape
    B = c.shape[0]
    tn = 1536
    rows = -(-B // SUBLANES) * SUBLANES
    c = jnp.pad(c, ((0, rows - B), (0, 0)))
    return pl.pallas_call(
        _mod_kernel,
        grid=(L, N // tn),
        in_specs=[pl.BlockSpec((rows, D), lambda l, j: (0, 0)),
                  pl.BlockSpec((None, D, tn), lambda l, j: (l, 0, j)),
                  pl.BlockSpec((None, 1, tn), lambda l, j: (l, 0, j))],
        out_specs=pl.BlockSpec((None, B, tn), lambda l, j: (l, 0, j)),
        out_shape=jax.ShapeDtypeStruct((L, B, N), F32),
        compiler_params=_cparams("parallel", "parallel"),
        name="modulation",
    )(c, w_mod, b_mod.reshape(L, 1, N))


def _rope_kernel(pos_ref, freq_ref, ck_ref, sk_ref):
    ang = pos_ref[...].astype(F32) * freq_ref[...]
    lane = lax.broadcasted_iota(jnp.int32, ang.shape, 1)
    cosv, sinv = jnp.cos(ang), jnp.sin(ang)
    half = QK_ROPE // 2
    in_rope = (lane >= QK_NOPE) & (lane < QK_NOPE + QK_ROPE)
    ck_ref[...] = jnp.where(in_rope, cosv, 0.0)
    sk_ref[...] = jnp.where(in_rope, jnp.where(lane < QK_NOPE + half, -sinv, sinv), 0.0)


def _rope_tables(positions):
    B, S = positions.shape
    half = QK_ROPE // 2
    freqs = ROPE_THETA ** (-jnp.arange(0, QK_ROPE, 2, dtype=F32) / QK_ROPE)
    freq_row = jnp.concatenate([jnp.zeros((QK_NOPE,), F32), freqs, freqs,
                                jnp.zeros((HEAD_PAD - QK_NOPE - 2 * half,), F32)]).reshape(1, HEAD_PAD)
    ts = min(S, 1024)
    out = jax.ShapeDtypeStruct((B, S, HEAD_PAD), F32)
    return pl.pallas_call(
        _rope_kernel,
        grid=(B, S // ts),
        in_specs=[pl.BlockSpec((None, ts, 1), lambda b, i: (b, i, 0)),
                  pl.BlockSpec((1, HEAD_PAD), lambda b, i: (0, 0))],
        out_specs=[pl.BlockSpec((None, ts, HEAD_PAD), lambda b, i: (b, i, 0))] * 2,
        out_shape=[out, out],
        compiler_params=_cparams("parallel", "parallel"),
        name="rope_tables",
    )(positions.reshape(B, S, 1), freq_row)


def _rope_partner(x, lane_in_head):
    half = QK_ROPE // 2
    n = x.shape[1]
    from_upper = pltpu.roll(x, n - half, axis=1)
    from_lower = pltpu.roll(x, half, axis=1)
    return jnp.where(lane_in_head < QK_NOPE + half, from_upper, from_lower)


def _attn_in_kernel(x_ref, mod_ref, n1g_ref, w1_ref, qng_ref, wq_ref, kvng_ref, wkv_ref,
                    ck_ref, sk_ref,
                    q_ref, k_ref, v_ref, pu_ref, z_ref, cb_ref):
    x = x_ref[...]
    sh1, sc1 = mod_ref[0:1, :], mod_ref[1:2, :]
    h = _rms(x, n1g_ref[...]) * (1.0 + sc1) + sh1
    p = jnp.dot(h.astype(BF16), w1_ref[...], preferred_element_type=F32)
    o_ckv = Q_RANK
    o_kr = o_ckv + KV_RANK
    o_pu = o_kr + HEAD_PAD
    width = pu_ref.shape[-1]
    o_cx, o_cb, o_cc = o_pu + width, o_pu + 2 * width, o_pu + 3 * width
    cqn = _rms(p[:, 0:o_ckv], qng_ref[...]).astype(BF16)
    ckvn = _rms(p[:, o_ckv:o_kr], kvng_ref[...]).astype(BF16)
    q = jnp.dot(cqn, wq_ref[...], preferred_element_type=F32)
    kvp = jnp.dot(ckvn, wkv_ref[...], preferred_element_type=F32)
    ck, sk = ck_ref[...], sk_ref[...]
    lane = lax.broadcasted_iota(jnp.int32, ck.shape, 1)
    cq_tab = jnp.where(lane < QK_NOPE, 1.0, ck)
    scale = (QK_NOPE + QK_ROPE) ** -0.5 * float(np.log2(np.e))
    kr = p[:, o_kr:o_pu]
    krope = kr * ck + _rope_partner(kr, lane) * sk
    q_swap = _rope_partner(q, jnp.tile(lane, (1, N_HEADS)))
    for hd in range(N_HEADS):
        sl = slice(hd * HEAD_PAD, (hd + 1) * HEAD_PAD)
        q_ref[hd] = ((q[:, sl] * cq_tab + q_swap[:, sl] * sk) * scale).astype(BF16)
        k_ref[hd] = (kvp[:, sl] + krope).astype(BF16)
    vb = kvp[:, N_HEADS * HEAD_PAD:].astype(BF16)
    ones = jnp.ones((vb.shape[0], LANES), BF16)
    pieces = []
    for pr in range(N_HEADS // 2):
        pieces += [vb[:, pr * 2 * V_HEAD:(pr + 1) * 2 * V_HEAD], ones]
    v_ref[...] = jnp.concatenate(pieces, axis=1)
    pu_ref[...] = p[:, o_pu:o_cx]
    z_ref[...] = p[:, o_cc:o_cc + width] * p[:, o_cx:o_cb]
    cb_ref[...] = p[:, o_cb:o_cc]


def _attn_in(x, mod, layer, n1g, w1, qng, wq, kvng, wkv, ck, sk, tm):
    B, S, D = x.shape
    width = (w1.shape[-1] - Q_RANK - KV_RANK - HEAD_PAD) // 4
    tok = lambda b, i: (b, i, 0)
    hspec = pl.BlockSpec((None, N_HEADS, tm, HEAD_PAD), lambda b, i: (b, 0, i, 0))
    wspec = lambda a: _layer_spec(a, layer)
    f32o = jax.ShapeDtypeStruct((B, S, width), F32)
    return pl.pallas_call(
        _attn_in_kernel,
        grid=(B, S // tm),
        in_specs=[pl.BlockSpec((None, tm, D), tok),
                  _mod_spec(mod, layer),
                  wspec(n1g), wspec(w1), wspec(qng), wspec(wq), wspec(kvng), wspec(wkv),
                  pl.BlockSpec((None, tm, HEAD_PAD), tok),
                  pl.BlockSpec((None, tm, HEAD_PAD), tok)],
        out_specs=[hspec, hspec,
                   pl.BlockSpec((None, tm, N_HEADS * V_EXT // 2), tok),
                   pl.BlockSpec((None, tm, width), tok),
                   pl.BlockSpec((None, tm, width), tok),
                   pl.BlockSpec((None, tm, width), tok)],
        out_shape=[jax.ShapeDtypeStruct((B, N_HEADS, S, HEAD_PAD), BF16),
                   jax.ShapeDtypeStruct((B, N_HEADS, S, HEAD_PAD), BF16),
                   jax.ShapeDtypeStruct((B, S, N_HEADS * V_EXT // 2), BF16),
                   f32o, f32o, f32o],
        compiler_params=_cparams("parallel", "parallel"),
        name="attn_in",
    )(x, mod, n1g, w1, qng, wq, kvng, wkv, ck, sk)


def _attention_kernel(q_ref, k_ref, v_ref, o_ref):
    nh = q_ref.shape[0]
    pair = 2 * V_HEAD
    scores = [lax.dot_general(q_ref[hd], k_ref[hd], (((1,), (1,)), ((), ())),
                              preferred_element_type=F32) for hd in range(nh)]
    outs = []
    for hd in range(nh):
        s = scores[hd]
        m = jnp.max(s, axis=-1, keepdims=True)
        p = jnp.exp2(s - m).astype(BF16)
        v = v_ref[:, (hd // 2) * V_EXT:(hd // 2 + 1) * V_EXT]
        r = jnp.dot(p, v, preferred_element_type=F32)
        outs.append(r[:, :pair] / r[:, pair:pair + 1])
    lane = lax.broadcasted_iota(jnp.int32, outs[0].shape, 1)
    for pr in range(nh // 2):
        o_ref[:, pr * pair:(pr + 1) * pair] = jnp.where(lane < V_HEAD, outs[2 * pr], outs[2 * pr + 1]).astype(BF16)


def _attention(q, k, v, tq, nh=4):
    B, H, S, _ = q.shape
    return pl.pallas_call(
        _attention_kernel,
        grid=(B, H // nh, S // tq),
        in_specs=[pl.BlockSpec((None, nh, tq, HEAD_PAD), lambda b, h, i: (b, h, i, 0)),
                  pl.BlockSpec((None, nh, S, HEAD_PAD), lambda b, h, i: (b, h, 0, 0)),
                  pl.BlockSpec((None, S, nh * V_EXT // 2), lambda b, h, i: (b, 0, h))],
        out_specs=pl.BlockSpec((None, tq, nh * V_HEAD), lambda b, h, i: (b, i, h)),
        out_shape=jax.ShapeDtypeStruct((B, S, H * V_HEAD), BF16),
        compiler_params=_cparams("parallel", "parallel", "parallel"),
        name="attention",
    )(q, k, v)


def _shift_rows(ext, shift, ts):
    n = ext.shape[0]
    return pltpu.roll(ext, shift % n, axis=0)[HALO:HALO + ts]


def _mix_post_kernel(x_ref, o_ref, pu_ref, pup_ref, pun_ref, z_ref, zp_ref, zn_ref, cb_ref,
                     mod_ref, n1g_ref, wg_ref, bg_ref, woa_ref, wpool_ref, pscale_ref, convw_ref, woc_ref,
                     wout_ref, n2g_ref, wrt_ref,
                     xo_ref, h2_ref, lg_ref, *, seq_len):
    i = pl.program_id(1)
    last = pl.num_programs(1) - 1
    ts, D = x_ref.shape
    sh1, sc1, g1 = mod_ref[0:1, :], mod_ref[1:2, :], mod_ref[2:3, :]
    sh2, sc2 = mod_ref[3:4, :], mod_ref[4:5, :]
    x = x_ref[...]
    h = (_rms(x, n1g_ref[...]) * (1.0 + sc1) + sh1).astype(BF16)
    gl = jnp.dot(h, wg_ref[...], preferred_element_type=F32) + bg_ref[...]
    gates = jax.nn.sigmoid(gl)
    ya = jnp.dot(o_ref[...], woa_ref[...], preferred_element_type=F32)

    first_f = jnp.where(i == 0, 0.0, 1.0)
    last_f = jnp.where(i == last, 0.0, 1.0)
    t = i * ts + lax.broadcasted_iota(jnp.int32, (ts, 1), 0)
    pu = pu_ref[...]
    pext = jnp.concatenate([pup_ref[...] * first_f, pu, pun_ref[...] * last_f], axis=0)
    yb_parts = []
    for gi, w in enumerate(POOL_WINDOWS):
        gs = slice(gi * POOL_GROUP, (gi + 1) * POOL_GROUP)
        a = pext[:, gs]
        span = 1
        while span < w:
            a = a + pltpu.roll(a, span, axis=0)
            span *= 2
        wsum = _shift_rows(a, -(w // 2 - 1), ts)
        lo = jnp.maximum(t - w // 2, 0)
        hi = jnp.minimum(t + w // 2 - 1, seq_len - 1)
        cnt = (hi - lo + 1).astype(F32)
        mixed = wsum / cnt - pu[:, gs]
        yb_parts.append(jnp.dot(mixed.astype(BF16), wpool_ref[gi], preferred_element_type=F32))
    yb = jnp.concatenate(yb_parts, axis=-1) * pscale_ref[...]

    z = z_ref[...]
    zext = jnp.concatenate([zp_ref[...] * first_f, z, zn_ref[...] * last_f], axis=0)
    yconv = (convw_ref[0:1, :] * _shift_rows(zext, 1, ts) + convw_ref[1:2, :] * z
             + convw_ref[2:3, :] * _shift_rows(zext, -1, ts))
    yc = jnp.dot((cb_ref[...] * yconv).astype(BF16), woc_ref[...], preferred_element_type=F32)

    merged = gates[:, 0:D] * ya + gates[:, D:2 * D] * yb + gates[:, 2 * D:3 * D] * yc
    xn = x + g1 * jnp.dot(merged.astype(BF16), wout_ref[...], preferred_element_type=F32)
    xo_ref[...] = xn
    h2 = _rms(xn, n2g_ref[...]) * (1.0 + sc2) + sh2
    h2_hi, h2_lo = _split_bf16(h2)
    h2_ref[...] = h2_hi
    wr_hi, wr_lo = _split_bf16(wrt_ref[...])
    n_e = wr_hi.shape[0]
    nt_dims = (((1,), (1,)), ((), ()))
    r_hi = lax.dot_general(jnp.concatenate([wr_hi, wr_lo], axis=0), h2_hi, nt_dims,
                           preferred_element_type=F32)
    r_lo = lax.dot_general(wr_hi, h2_lo, nt_dims, preferred_element_type=F32)
    lg_ref[...] = r_hi[:n_e] + r_hi[n_e:] + r_lo


def _mix_post(x, o, pu, z, cb, mod, layer, n1g, wg, bg, woa, wpool, pscale, convw, woc, wout, n2g, wrt, ts):
    B, S, D = x.shape
    W = pu.shape[-1]
    E = wrt.shape[1]
    nh = ts // HALO
    tok = lambda b, i: (b, i, 0)
    prev = lambda b, i: (b, jnp.maximum(i * nh - 1, 0), 0)
    nxt = lambda b, i: (b, jnp.minimum((i + 1) * nh, S // HALO - 1), 0)
    wspec = lambda a: _layer_spec(a, layer)
    return pl.pallas_call(
        functools.partial(_mix_post_kernel, seq_len=S),
        grid=(B, S // ts),
        in_specs=[pl.BlockSpec((None, ts, D), tok),
                  pl.BlockSpec((None, ts, o.shape[-1]), tok),
                  pl.BlockSpec((None, ts, W), tok),
                  pl.BlockSpec((None, HALO, W), prev),
                  pl.BlockSpec((None, HALO, W), nxt),
                  pl.BlockSpec((None, ts, W), tok),
                  pl.BlockSpec((None, HALO, W), prev),
                  pl.BlockSpec((None, HALO, W), nxt),
                  pl.BlockSpec((None, ts, W), tok),
                  _mod_spec(mod, layer),
                  wspec(n1g), wspec(wg), wspec(bg), wspec(woa), wspec(wpool), wspec(pscale), wspec(convw),
                  wspec(woc), wspec(wout), wspec(n2g), wspec(wrt)],
        out_specs=[pl.BlockSpec((None, ts, D), tok),
                   pl.BlockSpec((None, ts, D), tok),
                   pl.BlockSpec((None, E, ts), lambda b, i: (b, 0, i))],
        out_shape=[jax.ShapeDtypeStruct((B, S, D), F32),
                   jax.ShapeDtypeStruct((B, S, D), BF16),
                   jax.ShapeDtypeStruct((B, E, S), F32)],
        compiler_params=_cparams("parallel", "parallel"),
        name="mix_post",
    )(x, o, pu, pu, pu, z, z, z, cb, mod, n1g, wg, bg, woa, wpool, pscale, convw, woc, wout, n2g, wrt)


def _prefix_chunks(flags, tri):
    E, S = flags.shape
    running = jnp.zeros((E, 1), F32)
    pieces, starts = [], []
    for c in range(S // TOKEN_BLOCK):
        blk = flags[:, c * TOKEN_BLOCK:(c + 1) * TOKEN_BLOCK]
        incl = jnp.dot(blk.astype(BF16), tri, preferred_element_type=F32)
        pieces.append(incl - blk + running)
        starts.append(running)
        running = running + jnp.sum(blk, axis=-1, keepdims=True)
    starts.append(running)
    return jnp.concatenate(pieces, axis=-1), starts


def _route_kernel(lg_ref, slot_ref, wts_ref, cnt_ref, *, cap):
    lg = lg_ref[...]
    B, n_e, S = lg.shape
    m = jnp.max(lg, axis=1, keepdims=True)
    ex = jnp.exp(lg - m)
    aff = (ex / jnp.sum(ex, axis=1, keepdims=True)).reshape(B * n_e, S)
    E = B * n_e

    def step(it, tau_bits):
        cand = tau_bits | jnp.left_shift(jnp.int32(1), 30 - it)
        n = jnp.sum((aff >= lax.bitcast_convert_type(cand, F32)).astype(jnp.int32), axis=-1, keepdims=True)
        return jnp.where(n >= cap, cand, tau_bits)

    tau = lax.bitcast_convert_type(lax.fori_loop(0, 31, step, jnp.zeros((E, 1), jnp.int32)), F32)
    gt = aff > tau
    eq = aff == tau
    need = cap - jnp.sum(gt.astype(jnp.int32), axis=-1, keepdims=True)
    row = lax.broadcasted_iota(jnp.int32, (TOKEN_BLOCK, TOKEN_BLOCK), 0)
    col = lax.broadcasted_iota(jnp.int32, (TOKEN_BLOCK, TOKEN_BLOCK), 1)
    tri = jnp.where(row <= col, 1.0, 0.0).astype(BF16)
    eq_rank, _ = _prefix_chunks(jnp.where(eq, 1.0, 0.0), tri)
    sel = gt | (eq & (eq_rank.astype(jnp.int32) < need))
    pos, starts = _prefix_chunks(jnp.where(sel, 1.0, 0.0), tri)
    slot_ref[...] = jnp.where(sel, pos.astype(jnp.int32), -1).reshape(B, n_e, S)
    wts_ref[...] = jnp.where(sel, aff, 0.0).reshape(B, n_e, S)
    lane = lax.broadcasted_iota(jnp.int32, (E, LANES), 1)
    cnt = jnp.full((E, LANES), cap, jnp.int32)
    for c, st in enumerate(starts):
        cnt = jnp.where(lane == c, st.astype(jnp.int32), cnt)
    cnt_ref[...] = cnt.reshape(B, n_e, LANES)


def _route(logits_t, cap):
    B, E, S = logits_t.shape
    spec = pl.BlockSpec((B, E, S), lambda i: (0, 0, 0))
    return pl.pallas_call(
        functools.partial(_route_kernel, cap=cap),
        grid=(1,),
        in_specs=[spec],
        out_specs=[spec, spec, pl.BlockSpec((B, E, LANES), lambda i: (0, 0, 0))],
        out_shape=[jax.ShapeDtypeStruct((B, E, S), jnp.int32),
                   jax.ShapeDtypeStruct((B, E, S), F32),
                   jax.ShapeDtypeStruct((B, E, LANES), jnp.int32)],
        compiler_params=_cparams("arbitrary"),
        name="route",
    )(logits_t)


def _window_start(c0, cap):
    a0 = jnp.left_shift(jnp.right_shift(c0, SLOT_ALIGN_LOG2), SLOT_ALIGN_LOG2)
    return pl.multiple_of(jnp.minimum(a0, cap - SLOT_WINDOW), 1 << SLOT_ALIGN_LOG2)


def _moe_step_tokens(S, tiles):
    tb = min(S, tiles * TOKEN_BLOCK)
    assert S % tb == 0
    return tb


def _gather_kernel(cnt_ref, slot_ref, wts_ref, h2_ref, xg_ref, tv_ref):
    b, step = pl.program_id(0), pl.program_id(1)
    E, cap, D = xg_ref.shape
    win = SLOT_WINDOW
    tiles = h2_ref.shape[0] // TOKEN_BLOCK

    @pl.when(step == 0)
    def _():
        xg_ref[...] = jnp.zeros(xg_ref.shape, BF16)
        tv_ref[...] = jnp.zeros(tv_ref.shape, F32)

    p_iota = lax.broadcasted_iota(jnp.int32, (win, TOKEN_BLOCK), 0)

    def tile_refs(t):
        toks = slice(t * TOKEN_BLOCK, (t + 1) * TOKEN_BLOCK)
        c = step * tiles + t
        starts = [_window_start(cnt_ref[(b * E + e) * CNT_PAD + c], cap) for e in range(E)]
        return toks, c, starts

    def add_rows(e, a, hit, rows, toks):
        sl = pl.ds(a, win)
        xg_ref[e, sl, :] = (xg_ref[e, sl, :].astype(F32) + rows).astype(BF16)
        tv_ref[e, sl, :] += jnp.sum(jnp.where(hit, wts_ref[e:e + 1, toks], 0.0), axis=1, keepdims=True)

    for t in range(tiles):
        toks, c, starts = tile_refs(t)
        hits = [(p_iota + starts[e]) == slot_ref[e:e + 1, toks] for e in range(E)]
        onehot = jnp.concatenate([jnp.where(h, 1.0, 0.0).astype(BF16) for h in hits], axis=0)
        rows = jnp.dot(onehot, h2_ref[toks, :], preferred_element_type=F32)
        for e in range(E):
            add_rows(e, starts[e], hits[e], rows[e * win:(e + 1) * win], toks)

    for t in range(tiles):
        toks, c, starts = tile_refs(t)
        for e in range(E):
            last = cnt_ref[(b * E + e) * CNT_PAD + c + 1]

            @pl.when(last > starts[e] + win)
            def _():
                def more(k, carry):
                    lo = starts[e] + k * win
                    a = pl.multiple_of(jnp.minimum(lo, cap - win), 1 << SLOT_ALIGN_LOG2)
                    hit = ((p_iota + a) == slot_ref[e:e + 1, toks]) & ((p_iota + a) >= lo)
                    add_rows(e, a, hit, jnp.dot(jnp.where(hit, 1.0, 0.0).astype(BF16), h2_ref[toks, :],
                                                preferred_element_type=F32), toks)
                    return carry

                lax.fori_loop(1, (last - starts[e] + win - 1) // win, more, 0)


def _gather(cnt_flat, slot, wts, h2, cap):
    B, E, S = slot.shape
    D = h2.shape[-1]
    assert cap % SLOT_WINDOW == 0
    tb = _moe_step_tokens(S, GATHER_TILES_PER_STEP)
    tile = lambda b, c, cnt: (b, 0, c)
    return pl.pallas_call(
        _gather_kernel,
        grid_spec=pltpu.PrefetchScalarGridSpec(
            num_scalar_prefetch=1,
            grid=(B, S // tb),
            in_specs=[pl.BlockSpec((None, E, tb), tile),
                      pl.BlockSpec((None, E, tb), tile),
                      pl.BlockSpec((None, tb, D), lambda b, c, cnt: (b, c, 0))],
            out_specs=[pl.BlockSpec((None, E, cap, D), lambda b, c, cnt: (b, 0, 0, 0)),
                       pl.BlockSpec((None, E, cap, 1), lambda b, c, cnt: (b, 0, 0, 0))]),
        out_shape=[jax.ShapeDtypeStruct((B, E, cap, D), BF16),
                   jax.ShapeDtypeStruct((B, E, cap, 1), F32)],
        compiler_params=_cparams("parallel", "arbitrary"),
        name="gather",
    )(cnt_flat, slot, wts, h2)


def _experts_kernel(xg_ref, tv_ref, wg_ref, wu_ref, wd_ref, y_ref):
    wg, wu, wd = wg_ref[...].astype(BF16), wu_ref[...].astype(BF16), wd_ref[...].astype(BF16)
    for b in range(xg_ref.shape[0]):
        xg = xg_ref[b]
        g = jnp.dot(xg, wg, preferred_element_type=F32)
        u = jnp.dot(xg, wu, preferred_element_type=F32)
        a = (g * jax.nn.sigmoid(g) * u).astype(BF16)
        y_ref[b] = (jnp.dot(a, wd, preferred_element_type=F32) * tv_ref[b]).astype(BF16)


def _experts(xg, tv, w_gate, w_up, w_down, layer):
    B, E, cap, D = xg.shape
    F = w_gate.shape[-1]
    tok = pl.BlockSpec((B, None, cap, D), lambda e: (0, e, 0, 0))
    return pl.pallas_call(
        _experts_kernel,
        grid=(E,),
        in_specs=[tok,
                  pl.BlockSpec((B, None, cap, 1), lambda e: (0, e, 0, 0)),
                  pl.BlockSpec((None, None, D, F), lambda e: (layer, e, 0, 0)),
                  pl.BlockSpec((None, None, D, F), lambda e: (layer, e, 0, 0)),
                  pl.BlockSpec((None, None, F, D), lambda e: (layer, e, 0, 0))],
        out_specs=tok,
        out_shape=jax.ShapeDtypeStruct((B, E, cap, D), BF16),
        compiler_params=_cparams("parallel"),
        name="experts",
    )(xg, tv, w_gate, w_up, w_down)


def _combine_kernel(cnt_ref, slot_ref, y_ref, x_ref, mod_ref, fg_ref, o_ref, acc_ref, *, final_norm):
    b, step = pl.program_id(0), pl.program_id(1)
    E, cap, D = y_ref.shape
    win = SLOT_WINDOW
    tiles = x_ref.shape[0] // TOKEN_BLOCK
    p_iota = lax.broadcasted_iota(jnp.int32, (win, TOKEN_BLOCK), 0)
    contract_slots = (((0,), (0,)), ((), ()))

    def tile_refs(t):
        toks = slice(t * TOKEN_BLOCK, (t + 1) * TOKEN_BLOCK)
        c = step * tiles + t
        starts = [_window_start(cnt_ref[(b * E + e) * CNT_PAD + c], cap) for e in range(E)]
        return toks, c, starts

    for t in range(tiles):
        toks, c, starts = tile_refs(t)
        ycat = jnp.concatenate([y_ref[e, pl.ds(starts[e], win), :] for e in range(E)], axis=0)
        onehot = jnp.concatenate(
            [jnp.where((p_iota + starts[e]) == slot_ref[e:e + 1, toks], 1.0, 0.0).astype(BF16)
             for e in range(E)], axis=0)
        acc_ref[toks, :] = lax.dot_general(onehot, ycat, contract_slots, preferred_element_type=F32)

    for t in range(tiles):
        toks, c, starts = tile_refs(t)
        for e in range(E):
            last = cnt_ref[(b * E + e) * CNT_PAD + c + 1]

            @pl.when(last > starts[e] + win)
            def _():
                def more(k, carry):
                    lo = starts[e] + k * win
                    a = pl.multiple_of(jnp.minimum(lo, cap - win), 1 << SLOT_ALIGN_LOG2)
                    hit = ((p_iota + a) == slot_ref[e:e + 1, toks]) & ((p_iota + a) >= lo)
                    acc_ref[toks, :] += lax.dot_general(jnp.where(hit, 1.0, 0.0).astype(BF16),
                                                        y_ref[e, pl.ds(a, win), :], contract_slots,
                                                        preferred_element_type=F32)
                    return carry

                lax.fori_loop(1, (last - starts[e] + win - 1) // win, more, 0)

    out = x_ref[...] + mod_ref[5:6, :] * acc_ref[...]
    if final_norm:
        out = _rms(out, fg_ref[...])
    o_ref[...] = out


def _combine(cnt_flat, slot, y, x, mod, layer, final_g, final_norm):
    B, S, D = x.shape
    E, cap = y.shape[1], y.shape[2]
    tb = _moe_step_tokens(S, COMBINE_TILES_PER_STEP)
    tok = lambda b, c, cnt: (b, c, 0)
    return pl.pallas_call(
        functools.partial(_combine_kernel, final_norm=final_norm),
        grid_spec=pltpu.PrefetchScalarGridSpec(
            num_scalar_prefetch=1,
            grid=(B, S // tb),
            in_specs=[pl.BlockSpec((None, E, tb), lambda b, c, cnt: (b, 0, c)),
                      pl.BlockSpec((None, E, cap, D), lambda b, c, cnt: (b, 0, 0, 0)),
                      pl.BlockSpec((None, tb, D), tok),
                      _mod_spec(mod, layer),
                      pl.BlockSpec((1, D), lambda b, c, cnt: (0, 0))],
            out_specs=pl.BlockSpec((None, tb, D), tok),
            scratch_shapes=[pltpu.VMEM((tb, D), F32)]),
        out_shape=jax.ShapeDtypeStruct((B, S, D), F32),
        compiler_params=_cparams("parallel", "arbitrary"),
        name="combine",
    )(cnt_flat, slot, y, x, mod, final_g)


def _prep_weights(w_in, w_uq, w_ukv, w_oa, w_pool, w_oc, w_out, w_router, width):
    L, D, _ = w_in.shape
    o_ckv = Q_RANK
    o_kr = o_ckv + KV_RANK
    o_pu = o_kr + QK_ROPE
    o_gl = o_pu + 4 * width
    zeros = lambda *shape: jnp.zeros(shape, F32)
    wb = w_in.astype(BF16)
    zpad = lambda n: jnp.zeros((L, D, n), BF16)
    w1 = jnp.concatenate([wb[..., :o_kr], zpad(QK_NOPE), wb[..., o_kr:o_pu],
                          zpad(HEAD_PAD - QK_NOPE - QK_ROPE), wb[..., o_pu:o_gl]], axis=-1)
    wg = wb[..., o_gl:]

    dqk = QK_NOPE + QK_ROPE
    q4 = w_uq.reshape(L, Q_RANK, N_HEADS, dqk)
    wq = jnp.concatenate([q4, zeros(L, Q_RANK, N_HEADS, HEAD_PAD - dqk)], axis=-1)
    wq = wq.reshape(L, Q_RANK, N_HEADS * HEAD_PAD).astype(BF16)

    kv4 = w_ukv.reshape(L, KV_RANK, N_HEADS, QK_NOPE + V_HEAD)
    wk = jnp.concatenate([kv4[..., :QK_NOPE], zeros(L, KV_RANK, N_HEADS, HEAD_PAD - QK_NOPE)], axis=-1)
    wkv = jnp.concatenate([wk.reshape(L, KV_RANK, N_HEADS * HEAD_PAD),
                           kv4[..., QK_NOPE:].reshape(L, KV_RANK, N_HEADS * V_HEAD)], axis=-1).astype(BF16)
    return dict(w1=w1, wg=wg, wq=wq, wkv=wkv, woa=w_oa.astype(BF16), wpool=w_pool.astype(BF16),
                woc=w_oc.astype(BF16), wout=w_out.astype(BF16), wrt=w_router.transpose(0, 2, 1))


def kernel(x, c, positions, w_mod, b_mod, norm1_g, w_in, b_gate, q_norm_g, w_uq, kv_norm_g, w_ukv, w_oa, w_pool,
           pool_scale, conv_w, w_oc, w_out, norm2_g, w_router, w_gate, w_up, w_down, final_g):
    B, S, D = x.shape
    L = w_mod.shape[0]
    E = w_router.shape[-1]
    width = w_oc.shape[1]
    cap = EC_FACTOR * S // E
    assert S % TOKEN_BLOCK == 0 and S // TOKEN_BLOCK + 1 <= CNT_PAD
    tm = min(S, 512)
    tq = min(S, 256)

    mod = _modulation(c, w_mod, b_mod).reshape(L, B, 6, D)
    ck, sk = _rope_tables(positions)
    w = _prep_weights(w_in, w_uq, w_ukv, w_oa, w_pool, w_oc, w_out, w_router, width)
    rows = lambda a: a.reshape(L, 1, -1)
    n1g, n2g, qng, kvng = rows(norm1_g), rows(norm2_g), rows(q_norm_g), rows(kv_norm_g)
    bg, pscale = rows(b_gate), rows(pool_scale)
    for l in range(L):
        q, k, v, pu, z, cb = _attn_in(x, mod, l, n1g, w["w1"], qng, w["wq"], kvng, w["wkv"], ck, sk, tm)
        o = _attention(q, k, v, tq)
        x, h2, logits_t = _mix_post(x, o, pu, z, cb, mod, l, n1g, w["wg"], bg, w["woa"], w["wpool"], pscale,
                                    conv_w, w["woc"], w["wout"], n2g, w["wrt"], tm)
        slot, wts, cnt = _route(logits_t, cap)
        cnt_flat = cnt[:, :, :CNT_PAD].reshape(-1)
        xg, tv = _gather(cnt_flat, slot, wts, h2, cap)
        y = _experts(xg, tv, w_gate, w_up, w_down, l)
        x = _combine(cnt_flat, slot, y, x, mod, l, final_g.reshape(1, D), final_norm=(l == L - 1))
    return x
```

```python
import functools

import jax
import jax.numpy as jnp
import numpy as np
from jax import lax
from jax.experimental import pallas as pl
from jax.experimental.pallas import tpu as pltpu

N_HEADS = 8
QK_NOPE = 64
QK_ROPE = 32
V_HEAD = 64
Q_RANK = 384
KV_RANK = 256
ROPE_THETA = 10000.0
POOL_WINDOWS = (2, 4, 8, 16)
POOL_GROUP = 128
EC_FACTOR = 2
EPS = 1e-6

LANES = 128
SUBLANES = 8
HEAD_PAD = 128
V_EXT = 256
HALO = 16
SLOT_WINDOW = 64
SLOT_ALIGN_LOG2 = 4
TOKEN_BLOCK = 256
GATHER_TILES_PER_STEP = 8
COMBINE_TILES_PER_STEP = 2
CNT_PAD = 32
VMEM_LIMIT = 56 * 1024 * 1024

BF16 = jnp.bfloat16
F32 = jnp.float32


def _cparams(*sem):
    return pltpu.CompilerParams(dimension_semantics=sem, vmem_limit_bytes=VMEM_LIMIT)


def _layer_spec(a, layer):
    zeros = (0,) * (a.ndim - 1)
    return pl.BlockSpec((None,) + a.shape[1:], lambda *_: (layer,) + zeros)


def _mod_spec(mod, layer):
    return pl.BlockSpec((None, None) + mod.shape[2:], lambda b, *_: (layer, b, 0, 0))


def _rms(x, g):
    return x * lax.rsqrt(jnp.mean(x * x, axis=-1, keepdims=True) + EPS) * g


def _split_bf16(a):
    hi = a.astype(BF16)
    return hi, (a - hi.astype(F32)).astype(BF16)


def _mod_kernel(c_ref, w_ref, b_ref, o_ref):
    c = c_ref[...]
    c_hi, c_lo = _split_bf16(c * jax.nn.sigmoid(c))
    w_hi, w_lo = _split_bf16(w_ref[...])
    rows = c.shape[0]
    r = jnp.dot(jnp.concatenate([c_hi, c_lo], axis=0), w_hi, preferred_element_type=F32)
    out = r[:rows] + r[rows:] + jnp.dot(c_hi, w_lo, preferred_element_type=F32)
    o_ref[...] = out[:o_ref.shape[0]] + b_ref[...]


def _modulation(c, w_mod, b_mod):
    L, D, N = w_mod.shape
    B = c.shape[0]
    tn = 1536
    rows = -(-B // SUBLANES) * SUBLANES
    c = jnp.pad(c, ((0, rows - B), (0, 0)))
    return pl.pallas_call(
        _mod_kernel,
        grid=(L, N // tn),
        in_specs=[pl.BlockSpec((rows, D), lambda l, j: (0, 0)),
                  pl.BlockSpec((None, D, tn), lambda l, j: (l, 0, j)),
                  pl.BlockSpec((None, 1, tn), lambda l, j: (l, 0, j))],
        out_specs=pl.BlockSpec((None, B, tn), lambda l, j: (l, 0, j)),
        out_shape=jax.ShapeDtypeStruct((L, B, N), F32),
        compiler_params=_cparams("parallel", "parallel"),
        name="modulation",
    )(c, w_mod, b_mod.reshape(L, 1, N))


def _rope_kernel(pos_ref, freq_ref, ck_ref, sk_ref):
    ang = pos_ref[...].astype(F32) * freq_ref[...]
    lane = lax.broadcasted_iota(jnp.int32, ang.shape, 1)
    cosv, sinv = jnp.cos(ang), jnp.sin(ang)
    half = QK_ROPE // 2
    in_rope = (lane >= QK_NOPE) & (lane < QK_NOPE + QK_ROPE)
    ck_ref[...] = jnp.where(in_rope, cosv, 0.0)
    sk_ref[...] = jnp.where(in_rope, jnp.where(lane < QK_NOPE + half, -sinv, sinv), 0.0)


def _rope_tables(positions):
    B, S = positions.shape
    half = QK_ROPE // 2
    freqs = ROPE_THETA ** (-jnp.arange(0, QK_ROPE, 2, dtype=F32) / QK_ROPE)
    freq_row = jnp.concatenate([jnp.zeros((QK_NOPE,), F32), freqs, freqs,
                                jnp.zeros((HEAD_PAD - QK_NOPE - 2 * half,), F32)]).reshape(1, HEAD_PAD)
    ts = min(S, 1024)
    out = jax.ShapeDtypeStruct((B, S, HEAD_PAD), F32)
    return pl.pallas_call(
        _rope_kernel,
        grid=(B, S // ts),
        in_specs=[pl.BlockSpec((None, ts, 1), lambda b, i: (b, i, 0)),
                  pl.BlockSpec((1, HEAD_PAD), lambda b, i: (0, 0))],
        out_specs=[pl.BlockSpec((None, ts, HEAD_PAD), lambda b, i: (b, i, 0))] * 2,
        out_shape=[out, out],
        compiler_params=_cparams("parallel", "parallel"),
        name="rope_tables",
    )(positions.reshape(B, S, 1), freq_row)


def _rope_partner(x, lane_in_head):
    half = QK_ROPE // 2
    n = x.shape[1]
    from_upper = pltpu.roll(x, n - half, axis=1)
    from_lower = pltpu.roll(x, half, axis=1)
    return jnp.where(lane_in_head < QK_NOPE + half, from_upper, from_lower)


def _attn_in_kernel(x_ref, mod_ref, n1g_ref, w1_ref, qng_ref, wq_ref, kvng_ref, wkv_ref,
                    ck_ref, sk_ref,
                    q_ref, k_ref, v_ref, pu_ref, z_ref, cb_ref):
    x = x_ref[...]
    sh1, sc1 = mod_ref[0:1, :], mod_ref[1:2, :]
    h = _rms(x, n1g_ref[...]) * (1.0 + sc1) + sh1
    p = jnp.dot(h.astype(BF16), w1_ref[...], preferred_element_type=F32)
    o_ckv = Q_RANK
    o_kr = o_ckv + KV_RANK
    o_pu = o_kr + HEAD_PAD
    width = pu_ref.shape[-1]
    o_cx, o_cb, o_cc = o_pu + width, o_pu + 2 * width, o_pu + 3 * width
    cqn = _rms(p[:, 0:o_ckv], qng_ref[...]).astype(BF16)
    ckvn = _rms(p[:, o_ckv:o_kr], kvng_ref[...]).astype(BF16)
    q = jnp.dot(cqn, wq_ref[...], preferred_element_type=F32)
    kvp = jnp.dot(ckvn, wkv_ref[...], preferred_element_type=F32)
    ck, sk = ck_ref[...], sk_ref[...]
    lane = lax.broadcasted_iota(jnp.int32, ck.shape, 1)
    cq_tab = jnp.where(lane < QK_NOPE, 1.0, ck)
    scale = (QK_NOPE + QK_ROPE) ** -0.5 * float(np.log2(np.e))
    kr = p[:, o_kr:o_pu]
    krope = kr * ck + _rope_partner(kr, lane) * sk
    q_swap = _rope_partner(q, jnp.tile(lane, (1, N_HEADS)))
    for hd in range(N_HEADS):
        sl = slice(hd * HEAD_PAD, (hd + 1) * HEAD_PAD)
        q_ref[hd] = ((q[:, sl] * cq_tab + q_swap[:, sl] * sk) * scale).astype(BF16)
        k_ref[hd] = (kvp[:, sl] + krope).astype(BF16)
    vb = kvp[:, N_HEADS * HEAD_PAD:].astype(BF16)
    ones = jnp.ones((vb.shape[0], LANES), BF16)
    pieces = []
    for pr in range(N_HEADS // 2):
        pieces += [vb[:, pr * 2 * V_HEAD:(pr + 1) * 2 * V_HEAD], ones]
    v_ref[...] = jnp.concatenate(pieces, axis=1)
    pu_ref[...] = p[:, o_pu:o_cx]
    z_ref[...] = p[:, o_cc:o_cc + width] * p[:, o_cx:o_cb]
    cb_ref[...] = p[:, o_cb:o_cc]


def _attn_in(x, mod, layer, n1g, w1, qng, wq, kvng, wkv, ck, sk, tm):
    B, S, D = x.shape
    width = (w1.shape[-1] - Q_RANK - KV_RANK - HEAD_PAD) // 4
    tok = lambda b, i: (b, i, 0)
    hspec = pl.BlockSpec((None, N_HEADS, tm, HEAD_PAD), lambda b, i: (b, 0, i, 0))
    wspec = lambda a: _layer_spec(a, layer)
    f32o = jax.ShapeDtypeStruct((B, S, width), F32)
    return pl.pallas_call(
        _attn_in_kernel,
        grid=(B, S // tm),
        in_specs=[pl.BlockSpec((None, tm, D), tok),
                  _mod_spec(mod, layer),
                  wspec(n1g), wspec(w1), wspec(qng), wspec(wq), wspec(kvng), wspec(wkv),
                  pl.BlockSpec((None, tm, HEAD_PAD), tok),
                  pl.BlockSpec((None, tm, HEAD_PAD), tok)],
        out_specs=[hspec, hspec,
                   pl.BlockSpec((None, tm, N_HEADS * V_EXT // 2), tok),
                   pl.BlockSpec((None, tm, width), tok),
                   pl.BlockSpec((None, tm, width), tok),
                   pl.BlockSpec((None, tm, width), tok)],
        out_shape=[jax.ShapeDtypeStruct((B, N_HEADS, S, HEAD_PAD), BF16),
                   jax.ShapeDtypeStruct((B, N_HEADS, S, HEAD_PAD), BF16),
                   jax.ShapeDtypeStruct((B, S, N_HEADS * V_EXT // 2), BF16),
                   f32o, f32o, f32o],
        compiler_params=_cparams("parallel", "parallel"),
        name="attn_in",
    )(x, mod, n1g, w1, qng, wq, kvng, wkv, ck, sk)


def _attention_kernel(q_ref, k_ref, v_ref, o_ref):
    nh = q_ref.shape[0]
    pair = 2 * V_HEAD
    scores = [lax.dot_general(q_ref[hd], k_ref[hd], (((1,), (1,)), ((), ())),
                              preferred_element_type=F32) for hd in range(nh)]
    outs = []
    for hd in range(nh):
        s = scores[hd]
        m = jnp.max(s, axis=-1, keepdims=True)
        p = jnp.exp2(s - m).astype(BF16)
        v = v_ref[:, (hd // 2) * V_EXT:(hd // 2 + 1) * V_EXT]
        r = jnp.dot(p, v, preferred_element_type=F32)
        outs.append(r[:, :pair] / r[:, pair:pair + 1])
    lane = lax.broadcasted_iota(jnp.int32, outs[0].shape, 1)
    for pr in range(nh // 2):
        o_ref[:, pr * pair:(pr + 1) * pair] = jnp.where(lane < V_HEAD, outs[2 * pr], outs[2 * pr + 1]).astype(BF16)


def _attention(q, k, v, tq, nh=2):
    B, H, S, _ = q.shape
    return pl.pallas_call(
        _attention_kernel,
        grid=(B, H // nh, S // tq),
        in_specs=[pl.BlockSpec((None, nh, tq, HEAD_PAD), lambda b, h, i: (b, h, i, 0)),
                  pl.BlockSpec((None, nh, S, HEAD_PAD), lambda b, h, i: (b, h, 0, 0)),
                  pl.BlockSpec((None, S, nh * V_EXT // 2), lambda b, h, i: (b, 0, h))],
        out_specs=pl.BlockSpec((None, tq, nh * V_HEAD), lambda b, h, i: (b, i, h)),
        out_shape=jax.ShapeDtypeStruct((B, S, H * V_HEAD), BF16),
        compiler_params=_cparams("parallel", "parallel", "parallel"),
        name="attention",
    )(q, k, v)


def _shift_rows(ext, shift, ts):
    n = ext.shape[0]
    return pltpu.roll(ext, shift % n, axis=0)[HALO:HALO + ts]


def _mix_post_kernel(x_ref, o_ref, pu_ref, pup_ref, pun_ref, z_ref, zp_ref, zn_ref, cb_ref,
                     mod_ref, n1g_ref, wg_ref, bg_ref, woa_ref, wpool_ref, pscale_ref, convw_ref, woc_ref,
                     wout_ref, n2g_ref, wrt_ref,
                     xo_ref, h2_ref, lg_ref, *, seq_len):
    i = pl.program_id(1)
    last = pl.num_programs(1) - 1
    ts, D = x_ref.shape
    sh1, sc1, g1 = mod_ref[0:1, :], mod_ref[1:2, :], mod_ref[2:3, :]
    sh2, sc2 = mod_ref[3:4, :], mod_ref[4:5, :]
    x = x_ref[...]
    h = (_rms(x, n1g_ref[...]) * (1.0 + sc1) + sh1).astype(BF16)
    gl = jnp.dot(h, wg_ref[...], preferred_element_type=F32) + bg_ref[...]
    gates = jax.nn.sigmoid(gl)
    ya = jnp.dot(o_ref[...], woa_ref[...], preferred_element_type=F32)

    first_f = jnp.where(i == 0, 0.0, 1.0)
    last_f = jnp.where(i == last, 0.0, 1.0)
    t = i * ts + lax.broadcasted_iota(jnp.int32, (ts, 1), 0)
    pu = pu_ref[...]
    pext = jnp.concatenate([pup_ref[...] * first_f, pu, pun_ref[...] * last_f], axis=0)
    yb_parts = []
    for gi, w in enumerate(POOL_WINDOWS):
        gs = slice(gi * POOL_GROUP, (gi + 1) * POOL_GROUP)
        a = pext[:, gs]
        span = 1
        while span < w:
            a = a + pltpu.roll(a, span, axis=0)
            span *= 2
        wsum = _shift_rows(a, -(w // 2 - 1), ts)
        lo = jnp.maximum(t - w // 2, 0)
        hi = jnp.minimum(t + w // 2 - 1, seq_len - 1)
        cnt = (hi - lo + 1).astype(F32)
        mixed = wsum / cnt - pu[:, gs]
        yb_parts.append(jnp.dot(mixed.astype(BF16), wpool_ref[gi], preferred_element_type=F32))
    yb = jnp.concatenate(yb_parts, axis=-1) * pscale_ref[...]

    z = z_ref[...]
    zext = jnp.concatenate([zp_ref[...] * first_f, z, zn_ref[...] * last_f], axis=0)
    yconv = (convw_ref[0:1, :] * _shift_rows(zext, 1, ts) + convw_ref[1:2, :] * z
             + convw_ref[2:3, :] * _shift_rows(zext, -1, ts))
    yc = jnp.dot((cb_ref[...] * yconv).astype(BF16), woc_ref[...], preferred_element_type=F32)

    merged = gates[:, 0:D] * ya + gates[:, D:2 * D] * yb + gates[:, 2 * D:3 * D] * yc
    xn = x + g1 * jnp.dot(merged.astype(BF16), wout_ref[...], preferred_element_type=F32)
    xo_ref[...] = xn
    h2 = _rms(xn, n2g_ref[...]) * (1.0 + sc2) + sh2
    h2_hi, h2_lo = _split_bf16(h2)
    h2_ref[...] = h2_hi
    wr_hi, wr_lo = _split_bf16(wrt_ref[...])
    n_e = wr_hi.shape[0]
    nt_dims = (((1,), (1,)), ((), ()))
    r_hi = lax.dot_general(jnp.concatenate([wr_hi, wr_lo], axis=0), h2_hi, nt_dims,
                           preferred_element_type=F32)
    r_lo = lax.dot_general(wr_hi, h2_lo, nt_dims, preferred_element_type=F32)
    lg_ref[...] = r_hi[:n_e] + r_hi[n_e:] + r_lo


def _mix_post(x, o, pu, z, cb, mod, layer, n1g, wg, bg, woa, wpool, pscale, convw, woc, wout, n2g, wrt, ts):
    B, S, D = x.shape
    W = pu.shape[-1]
    E = wrt.shape[1]
    nh = ts // HALO
    tok = lambda b, i: (b, i, 0)
    prev = lambda b, i: (b, jnp.maximum(i * nh - 1, 0), 0)
    nxt = lambda b, i: (b, jnp.minimum((i + 1) * nh, S // HALO - 1), 0)
    wspec = lambda a: _layer_spec(a, layer)
    return pl.pallas_call(
        functools.partial(_mix_post_kernel, seq_len=S),
        grid=(B, S // ts),
        in_specs=[pl.BlockSpec((None, ts, D), tok),
                  pl.BlockSpec((None, ts, o.shape[-1]), tok),
                  pl.BlockSpec((None, ts, W), tok),
                  pl.BlockSpec((None, HALO, W), prev),
                  pl.BlockSpec((None, HALO, W), nxt),
                  pl.BlockSpec((None, ts, W), tok),
                  pl.BlockSpec((None, HALO, W), prev),
                  pl.BlockSpec((None, HALO, W), nxt),
                  pl.BlockSpec((None, ts, W), tok),
                  _mod_spec(mod, layer),
                  wspec(n1g), wspec(wg), wspec(bg), wspec(woa), wspec(wpool), wspec(pscale), wspec(convw),
                  wspec(woc), wspec(wout), wspec(n2g), wspec(wrt)],
        out_specs=[pl.BlockSpec((None, ts, D), tok),
                   pl.BlockSpec((None, ts, D), tok),
                   pl.BlockSpec((None, E, ts), lambda b, i: (b, 0, i))],
        out_shape=[jax.ShapeDtypeStruct((B, S, D), F32),
                   jax.ShapeDtypeStruct((B, S, D), BF16),
                   jax.ShapeDtypeStruct((B, E, S), F32)],
        compiler_params=_cparams("parallel", "parallel"),
        name="mix_post",
    )(x, o, pu, pu, pu, z, z, z, cb, mod, n1g, wg, bg, woa, wpool, pscale, convw, woc, wout, n2g, wrt)


def _prefix_chunks(flags, tri):
    E, S = flags.shape
    running = jnp.zeros((E, 1), F32)
    pieces, starts = [], []
    for c in range(S // TOKEN_BLOCK):
        blk = flags[:, c * TOKEN_BLOCK:(c + 1) * TOKEN_BLOCK]
        incl = jnp.dot(blk.astype(BF16), tri, preferred_element_type=F32)
        pieces.append(incl - blk + running)
        starts.append(running)
        running = running + jnp.sum(blk, axis=-1, keepdims=True)
    starts.append(running)
    return jnp.concatenate(pieces, axis=-1), starts


def _route_kernel(lg_ref, slot_ref, wts_ref, cnt_ref, *, cap):
    lg = lg_ref[...]
    B, n_e, S = lg.shape
    m = jnp.max(lg, axis=1, keepdims=True)
    ex = jnp.exp(lg - m)
    aff = (ex / jnp.sum(ex, axis=1, keepdims=True)).reshape(B * n_e, S)
    E = B * n_e

    def step(it, tau_bits):
        cand = tau_bits | jnp.left_shift(jnp.int32(1), 30 - it)
        n = jnp.sum((aff >= lax.bitcast_convert_type(cand, F32)).astype(jnp.int32), axis=-1, keepdims=True)
        return jnp.where(n >= cap, cand, tau_bits)

    tau = lax.bitcast_convert_type(lax.fori_loop(0, 31, step, jnp.zeros((E, 1), jnp.int32)), F32)
    gt = aff > tau
    eq = aff == tau
    need = cap - jnp.sum(gt.astype(jnp.int32), axis=-1, keepdims=True)
    row = lax.broadcasted_iota(jnp.int32, (TOKEN_BLOCK, TOKEN_BLOCK), 0)
    col = lax.broadcasted_iota(jnp.int32, (TOKEN_BLOCK, TOKEN_BLOCK), 1)
    tri = jnp.where(row <= col, 1.0, 0.0).astype(BF16)
    eq_rank, _ = _prefix_chunks(jnp.where(eq, 1.0, 0.0), tri)
    sel = gt | (eq & (eq_rank.astype(jnp.int32) < need))
    pos, starts = _prefix_chunks(jnp.where(sel, 1.0, 0.0), tri)
    slot_ref[...] = jnp.where(sel, pos.astype(jnp.int32), -1).reshape(B, n_e, S)
    wts_ref[...] = jnp.where(sel, aff, 0.0).reshape(B, n_e, S)
    lane = lax.broadcasted_iota(jnp.int32, (E, LANES), 1)
    cnt = jnp.full((E, LANES), cap, jnp.int32)
    for c, st in enumerate(starts):
        cnt = jnp.where(lane == c, st.astype(jnp.int32), cnt)
    cnt_ref[...] = cnt.reshape(B, n_e, LANES)


def _route(logits_t, cap):
    B, E, S = logits_t.shape
    spec = pl.BlockSpec((B, E, S), lambda i: (0, 0, 0))
    return pl.pallas_call(
        functools.partial(_route_kernel, cap=cap),
        grid=(1,),
        in_specs=[spec],
        out_specs=[spec, spec, pl.BlockSpec((B, E, LANES), lambda i: (0, 0, 0))],
        out_shape=[jax.ShapeDtypeStruct((B, E, S), jnp.int32),
                   jax.ShapeDtypeStruct((B, E, S), F32),
                   jax.ShapeDtypeStruct((B, E, LANES), jnp.int32)],
        compiler_params=_cparams("arbitrary"),
        name="route",
    )(logits_t)


def _window_start(c0, cap):
    a0 = jnp.left_shift(jnp.right_shift(c0, SLOT_ALIGN_LOG2), SLOT_ALIGN_LOG2)
    return pl.multiple_of(jnp.minimum(a0, cap - SLOT_WINDOW), 1 << SLOT_ALIGN_LOG2)


def _moe_step_tokens(S, tiles):
    tb = min(S, tiles * TOKEN_BLOCK)
    assert S % tb == 0
    return tb


def _gather_kernel(cnt_ref, slot_ref, wts_ref, h2_ref, xg_ref, tv_ref):
    b, step = pl.program_id(0), pl.program_id(1)
    E, cap, D = xg_ref.shape
    win = SLOT_WINDOW
    tiles = h2_ref.shape[0] // TOKEN_BLOCK

    @pl.when(step == 0)
    def _():
        xg_ref[...] = jnp.zeros(xg_ref.shape, BF16)
        tv_ref[...] = jnp.zeros(tv_ref.shape, F32)

    p_iota = lax.broadcasted_iota(jnp.int32, (win, TOKEN_BLOCK), 0)

    def tile_refs(t):
        toks = slice(t * TOKEN_BLOCK, (t + 1) * TOKEN_BLOCK)
        c = step * tiles + t
        starts = [_window_start(cnt_ref[(b * E + e) * CNT_PAD + c], cap) for e in range(E)]
        return toks, c, starts

    def add_rows(e, a, hit, rows, toks):
        sl = pl.ds(a, win)
        xg_ref[e, sl, :] = (xg_ref[e, sl, :].astype(F32) + rows).astype(BF16)
        tv_ref[e, sl, :] += jnp.sum(jnp.where(hit, wts_ref[e:e + 1, toks], 0.0), axis=1, keepdims=True)

    for t in range(tiles):
        toks, c, starts = tile_refs(t)
        hits = [(p_iota + starts[e]) == slot_ref[e:e + 1, toks] for e in range(E)]
        onehot = jnp.concatenate([jnp.where(h, 1.0, 0.0).astype(BF16) for h in hits], axis=0)
        rows = jnp.dot(onehot, h2_ref[toks, :], preferred_element_type=F32)
        for e in range(E):
            add_rows(e, starts[e], hits[e], rows[e * win:(e + 1) * win], toks)

    for t in range(tiles):
        toks, c, starts = tile_refs(t)
        for e in range(E):
            last = cnt_ref[(b * E + e) * CNT_PAD + c + 1]

            @pl.when(last > starts[e] + win)
            def _():
                def more(k, carry):
                    lo = starts[e] + k * win
                    a = pl.multiple_of(jnp.minimum(lo, cap - win), 1 << SLOT_ALIGN_LOG2)
                    hit = ((p_iota + a) == slot_ref[e:e + 1, toks]) & ((p_iota + a) >= lo)
                    add_rows(e, a, hit, jnp.dot(jnp.where(hit, 1.0, 0.0).astype(BF16), h2_ref[toks, :],
                                                preferred_element_type=F32), toks)
                    return carry

                lax.fori_loop(1, (last - starts[e] + win - 1) // win, more, 0)


def _gather(cnt_flat, slot, wts, h2, cap):
    B, E, S = slot.shape
    D = h2.shape[-1]
    assert cap % SLOT_WINDOW == 0
    tb = _moe_step_tokens(S, GATHER_TILES_PER_STEP)
    tile = lambda b, c, cnt: (b, 0, c)
    return pl.pallas_call(
        _gather_kernel,
        grid_spec=pltpu.PrefetchScalarGridSpec(
            num_scalar_prefetch=1,
            grid=(B, S // tb),
            in_specs=[pl.BlockSpec((None, E, tb), tile),
                      pl.BlockSpec((None, E, tb), tile),
                      pl.BlockSpec((None, tb, D), lambda b, c, cnt: (b, c, 0))],
            out_specs=[pl.BlockSpec((None, E, cap, D), lambda b, c, cnt: (b, 0, 0, 0)),
                       pl.BlockSpec((None, E, cap, 1), lambda b, c, cnt: (b, 0, 0, 0))]),
        out_shape=[jax.ShapeDtypeStruct((B, E, cap, D), BF16),
                   jax.ShapeDtypeStruct((B, E, cap, 1), F32)],
        compiler_params=_cparams("parallel", "arbitrary"),
        name="gather",
    )(cnt_flat, slot, wts, h2)


def _experts_kernel(xg_ref, tv_ref, wg_ref, wu_ref, wd_ref, y_ref):
    wg, wu, wd = wg_ref[...].astype(BF16), wu_ref[...].astype(BF16), wd_ref[...].astype(BF16)
    for b in range(xg_ref.shape[0]):
        xg = xg_ref[b]
        g = jnp.dot(xg, wg, preferred_element_type=F32)
        u = jnp.dot(xg, wu, preferred_element_type=F32)
        a = (g * jax.nn.sigmoid(g) * u).astype(BF16)
        y_ref[b] = (jnp.dot(a, wd, preferred_element_type=F32) * tv_ref[b]).astype(BF16)


def _experts(xg, tv, w_gate, w_up, w_down, layer):
    B, E, cap, D = xg.shape
    F = w_gate.shape[-1]
    tok = pl.BlockSpec((B, None, cap, D), lambda e: (0, e, 0, 0))
    return pl.pallas_call(
        _experts_kernel,
        grid=(E,),
        in_specs=[tok,
                  pl.BlockSpec((B, None, cap, 1), lambda e: (0, e, 0, 0)),
                  pl.BlockSpec((None, None, D, F), lambda e: (layer, e, 0, 0)),
                  pl.BlockSpec((None, None, D, F), lambda e: (layer, e, 0, 0)),
                  pl.BlockSpec((None, None, F, D), lambda e: (layer, e, 0, 0))],
        out_specs=tok,
        out_shape=jax.ShapeDtypeStruct((B, E, cap, D), BF16),
        compiler_params=_cparams("parallel"),
        name="experts",
    )(xg, tv, w_gate, w_up, w_down)


def _combine_kernel(cnt_ref, slot_ref, y_ref, x_ref, mod_ref, fg_ref, o_ref, acc_ref, *, final_norm):
    b, step = pl.program_id(0), pl.program_id(1)
    E, cap, D = y_ref.shape
    win = SLOT_WINDOW
    tiles = x_ref.shape[0] // TOKEN_BLOCK
    p_iota = lax.broadcasted_iota(jnp.int32, (win, TOKEN_BLOCK), 0)
    contract_slots = (((0,), (0,)), ((), ()))

    def tile_refs(t):
        toks = slice(t * TOKEN_BLOCK, (t + 1) * TOKEN_BLOCK)
        c = step * tiles + t
        starts = [_window_start(cnt_ref[(b * E + e) * CNT_PAD + c], cap) for e in range(E)]
        return toks, c, starts

    for t in range(tiles):
        toks, c, starts = tile_refs(t)
        ycat = jnp.concatenate([y_ref[e, pl.ds(starts[e], win), :] for e in range(E)], axis=0)
        onehot = jnp.concatenate(
            [jnp.where((p_iota + starts[e]) == slot_ref[e:e + 1, toks], 1.0, 0.0).astype(BF16)
             for e in range(E)], axis=0)
        acc_ref[toks, :] = lax.dot_general(onehot, ycat, contract_slots, preferred_element_type=F32)

    for t in range(tiles):
        toks, c, starts = tile_refs(t)
        for e in range(E):
            last = cnt_ref[(b * E + e) * CNT_PAD + c + 1]

            @pl.when(last > starts[e] + win)
            def _():
                def more(k, carry):
                    lo = starts[e] + k * win
                    a = pl.multiple_of(jnp.minimum(lo, cap - win), 1 << SLOT_ALIGN_LOG2)
                    hit = ((p_iota + a) == slot_ref[e:e + 1, toks]) & ((p_iota + a) >= lo)
                    acc_ref[toks, :] += lax.dot_general(jnp.where(hit, 1.0, 0.0).astype(BF16),
                                                        y_ref[e, pl.ds(a, win), :], contract_slots,
                                                        preferred_element_type=F32)
                    return carry

                lax.fori_loop(1, (last - starts[e] + win - 1) // win, more, 0)

    out = x_ref[...] + mod_ref[5:6, :] * acc_ref[...]
    if final_norm:
        out = _rms(out, fg_ref[...])
    o_ref[...] = out


def _combine(cnt_flat, slot, y, x, mod, layer, final_g, final_norm):
    B, S, D = x.shape
    E, cap = y.shape[1], y.shape[2]
    tb = _moe_step_tokens(S, COMBINE_TILES_PER_STEP)
    tok = lambda b, c, cnt: (b, c, 0)
    return pl.pallas_call(
        functools.partial(_combine_kernel, final_norm=final_norm),
        grid_spec=pltpu.PrefetchScalarGridSpec(
            num_scalar_prefetch=1,
            grid=(B, S // tb),
            in_specs=[pl.BlockSpec((None, E, tb), lambda b, c, cnt: (b, 0, c)),
                      pl.BlockSpec((None, E, cap, D), lambda b, c, cnt: (b, 0, 0, 0)),
                      pl.BlockSpec((None, tb, D), tok),
                      _mod_spec(mod, layer),
                      pl.BlockSpec((1, D), lambda b, c, cnt: (0, 0))],
            out_specs=pl.BlockSpec((None, tb, D), tok),
            scratch_shapes=[pltpu.VMEM((tb, D), F32)]),
        out_shape=jax.ShapeDtypeStruct((B, S, D), F32),
        compiler_params=_cparams("parallel", "arbitrary"),
        name="combine",
    )(cnt_flat, slot, y, x, mod, final_g)


def _prep_weights(w_in, w_uq, w_ukv, w_oa, w_pool, w_oc, w_out, w_router, width):
    L, D, _ = w_in.shape
    o_ckv = Q_RANK
    o_kr = o_ckv + KV_RANK
    o_pu = o_kr + QK_ROPE
    o_gl = o_pu + 4 * width
    zeros = lambda *shape: jnp.zeros(shape, F32)
    wb = w_in.astype(BF16)
    zpad = lambda n: jnp.zeros((L, D, n), BF16)
    w1 = jnp.concatenate([wb[..., :o_kr], zpad(QK_NOPE), wb[..., o_kr:o_pu],
                          zpad(HEAD_PAD - QK_NOPE - QK_ROPE), wb[..., o_pu:o_gl]], axis=-1)
    wg = wb[..., o_gl:]

    dqk = QK_NOPE + QK_ROPE
    q4 = w_uq.reshape(L, Q_RANK, N_HEADS, dqk)
    wq = jnp.concatenate([q4, zeros(L, Q_RANK, N_HEADS, HEAD_PAD - dqk)], axis=-1)
    wq = wq.reshape(L, Q_RANK, N_HEADS * HEAD_PAD).astype(BF16)

    kv4 = w_ukv.reshape(L, KV_RANK, N_HEADS, QK_NOPE + V_HEAD)
    wk = jnp.concatenate([kv4[..., :QK_NOPE], zeros(L, KV_RANK, N_HEADS, HEAD_PAD - QK_NOPE)], axis=-1)
    wkv = jnp.concatenate([wk.reshape(L, KV_RANK, N_HEADS * HEAD_PAD),
                           kv4[..., QK_NOPE:].reshape(L, KV_RANK, N_HEADS * V_HEAD)], axis=-1).astype(BF16)
    return dict(w1=w1, wg=wg, wq=wq, wkv=wkv, woa=w_oa.astype(BF16), wpool=w_pool.astype(BF16),
                woc=w_oc.astype(BF16), wout=w_out.astype(BF16), wrt=w_router.transpose(0, 2, 1))


def kernel(x, c, positions, w_mod, b_mod, norm1_g, w_in, b_gate, q_norm_g, w_uq, kv_norm_g, w_ukv, w_oa, w_pool,
           pool_scale, conv_w, w_oc, w_out, norm2_g, w_router, w_gate, w_up, w_down, final_g):
    B, S, D = x.shape
    L = w_mod.shape[0]
    E = w_router.shape[-1]
    width = w_oc.shape[1]
    cap = EC_FACTOR * S // E
    assert S % TOKEN_BLOCK == 0 and S // TOKEN_BLOCK + 1 <= CNT_PAD
    tm = min(S, 512)
    tq = min(S, 512)

    mod = _modulation(c, w_mod, b_mod).reshape(L, B, 6, D)
    ck, sk = _rope_tables(positions)
    w = _prep_weights(w_in, w_uq, w_ukv, w_oa, w_pool, w_oc, w_out, w_router, width)
    rows = lambda a: a.reshape(L, 1, -1)
    n1g, n2g, qng, kvng = rows(norm1_g), rows(norm2_g), rows(q_norm_g), rows(kv_norm_g)
    bg, pscale = rows(b_gate), rows(pool_scale)
    for l in range(L):
        q, k, v, pu, z, cb = _attn_in(x, mod, l, n1g, w["w1"], qng, w["wq"], kvng, w["wkv"], ck, sk, tm)
        o = _attention(q, k, v, tq)
        x, h2, logits_t = _mix_post(x, o, pu, z, cb, mod, l, n1g, w["wg"], bg, w["woa"], w["wpool"], pscale,
                                    conv_w, w["woc"], w["wout"], n2g, w["wrt"], tm)
        slot, wts, cnt = _route(logits_t, cap)
        cnt_flat = cnt[:, :, :CNT_PAD].reshape(-1)
        xg, tv = _gather(cnt_flat, slot, wts, h2, cap)
        y = _experts(xg, tv, w_gate, w_up, w_down, l)
        x = _combine(cnt_flat, slot, y, x, mod, l, final_g.reshape(1, D), final_norm=(l == L - 1))
    return x
```
